```python
import math
import jax, jax.numpy as jnp
from jax import lax
import numpy as np

D_MODEL = 4096
BATCH = 1
SEQ = 16384
DEPTH = 1

ATTN_HEAD_DIM = 128
ATTN_HEADS = (3 * D_MODEL // 8) // ATTN_HEAD_DIM
ATTN_WIDTH = ATTN_HEADS * ATTN_HEAD_DIM
ATTN_BLOCK = 128
SSM_GROUP = 16
SSM_WIDTH = 3 * D_MODEL // 8
SSM_GROUPS = SSM_WIDTH // SSM_GROUP
SSM_STATE = 64
SSM_CHUNK = 128
MEM_TOKENS = 256
MEM_HEADS = 4
MEM_HEAD_DIM = D_MODEL // 16
MEM_WIDTH = MEM_HEADS * MEM_HEAD_DIM
N_BRANCHES = 3
IN_SPLITS = (ATTN_WIDTH, ATTN_WIDTH, ATTN_WIDTH, ATTN_HEADS, SSM_WIDTH, MEM_WIDTH, N_BRANCHES * D_MODEL)
IN_COLS = sum(IN_SPLITS)
N_EXPERTS = 32
TOP_K = 4
D_FF = 3 * D_MODEL // 8
SWIGLU_ALPHA = 1.702
SWIGLU_LIMIT = 7.0
MOE_BLOCK = 512
RMS_EPS = 1e-6

kernel_name = "hybrid_fox_s5_memxattn_moe_block"


def rmsnorm(x, g):
    xf = x.astype(jnp.float32)
    y = xf * lax.rsqrt(jnp.mean(xf * xf, axis=-1, keepdims=True) + RMS_EPS) * g.astype(jnp.float32)
    return y.astype(x.dtype)


def fox_attention(q, k, v, logf):
    b_, s_, h_, dh = q.shape
    nb = s_ // ATTN_BLOCK
    c = jnp.cumsum(logf, axis=1)
    def blocks(t):
        return t.reshape(b_, nb, ATTN_BLOCK, h_, dh).transpose(1, 0, 3, 2, 4)
    qb, kb, vb = blocks(q), blocks(k), blocks(v)
    cb = c.reshape(b_, nb, ATTN_BLOCK, h_).transpose(1, 0, 3, 2)
    scale = dh ** -0.5
    pos = jnp.arange(ATTN_BLOCK)

    def q_block(args):
        i, q_i, c_i = args
        q_pos = i * ATTN_BLOCK + pos

        def body(j, carry):
            m, l, acc = carry
            k_j, v_j, c_j = kb[j], vb[j], cb[j]
            s = jnp.einsum('bhqd,bhkd->bhqk', q_i, k_j, preferred_element_type=jnp.float32) * scale
            s = s + (c_i[..., :, None] - c_j[..., None, :])
            causal = (j * ATTN_BLOCK + pos)[None, :] <= q_pos[:, None]
            s = jnp.where(causal, s, -jnp.inf)
            m_new = jnp.maximum(m, jnp.max(s, axis=-1))
            p = jnp.exp(s - m_new[..., None])
            corr = jnp.exp(m - m_new)
            l = l * corr + jnp.sum(p, axis=-1)
            acc = acc * corr[..., None] + jnp.einsum('bhqk,bhkd->bhqd', p, v_j.astype(jnp.float32))
            return m_new, l, acc

        init = (jnp.full(c_i.shape, -jnp.inf, jnp.float32),
                jnp.zeros(c_i.shape, jnp.float32),
                jnp.zeros(q_i.shape, jnp.float32))
        _, l, acc = lax.fori_loop(0, i + 1, body, init)
        return (acc / l[..., None]).astype(q.dtype)

    out = lax.map(q_block, (jnp.arange(nb), qb, cb))
    return out.transpose(1, 0, 3, 2, 4).reshape(b_, s_, h_ * dh)


def s5_mixer(u, lam_re, lam_im, log_step, b_re, b_im, c_re, c_im, d_skip, w_glu, b_glu):
    b_, s_, p_ = u.shape
    f32 = jnp.float32
    lam = lax.complex(lam_re.astype(f32), lam_im.astype(f32))
    step = jnp.exp(log_step.astype(f32))[:, None]
    lam_bar = jnp.exp(lam * step)
    bmat = lax.complex(b_re.astype(f32), b_im.astype(f32))
    b_bar = ((lam_bar - 1.0) / lam)[..., None] * bmat
    cmat = lax.complex(c_re.astype(f32), c_im.astype(f32))
    nc = s_ // SSM_CHUNK
    uf = u.astype(f32)
    uc = uf.reshape(b_, nc, SSM_CHUNK, SSM_GROUPS, SSM_GROUP).transpose(1, 0, 2, 3, 4)
    a_chunk = jnp.broadcast_to(lam_bar, (b_, SSM_CHUNK, SSM_GROUPS, SSM_STATE))

    def combine(left, right):
        a1, h1 = left
        a2, h2 = right
        return a1 * a2, a2 * h1 + h2

    def chunk_step(h_prev, u_blk):
        bu = jnp.einsum('blgc,gnc->blgn', u_blk.astype(jnp.complex64), b_bar)
        a_cum, h = lax.associative_scan(combine, (a_chunk, bu), axis=1)
        h = h + a_cum * h_prev[:, None]
        y = jnp.einsum('blgn,gcn->blgc', h, cmat).real
        return h[:, -1], y

    h0 = jnp.zeros((b_, SSM_GROUPS, SSM_STATE), jnp.complex64)
    _, y = lax.scan(chunk_step, h0, uc)
    y = y.transpose(1, 0, 2, 3, 4).reshape(b_, s_, p_) + d_skip.astype(f32) * uf
    y = jax.nn.gelu(y)
    y = y * jax.nn.sigmoid(y @ w_glu.astype(f32) + b_glu.astype(f32))
    return y.astype(u.dtype)


def mem_attention(q, k, v):
    b_, s_, h_, dm = q.shape
    s = jnp.einsum('bshd,bmhd->bhsm', q, k, preferred_element_type=jnp.float32) * (dm ** -0.5)
    p = jax.nn.softmax(s, axis=-1)
    o = jnp.einsum('bhsm,bmhd->bshd', p, v.astype(jnp.float32))
    return o.reshape(b_, s_, h_ * dm).astype(q.dtype)


def moe_ffn(xn, w_router, b_router, w1, b1, w2, b2):
    b_, s_, d_ = xn.shape
    t_ = b_ * s_
    xt = xn.reshape(t_, d_)
    logits = (xt @ w_router).astype(jnp.float32) + b_router.astype(jnp.float32)
    top_val, top_idx = lax.top_k(logits, TOP_K)
    gate = jax.nn.softmax(top_val, axis=-1)
    n_assign = t_ * TOP_K
    flat_e = top_idx.reshape(-1)
    flat_tok = jnp.arange(n_assign, dtype=jnp.int32) // TOP_K
    flat_w = gate.reshape(-1)
    order = jnp.argsort(flat_e)
    sorted_e, sorted_tok, sorted_w = flat_e[order], flat_tok[order], flat_w[order]
    counts = jnp.bincount(flat_e, length=N_EXPERTS)
    padded = (counts + MOE_BLOCK - 1) // MOE_BLOCK * MOE_BLOCK
    pend = jnp.cumsum(padded)
    pstart = pend - padded
    ustart = jnp.cumsum(counts) - counts
    dest = pstart[sorted_e] + jnp.arange(n_assign) - ustart[sorted_e]
    n_blocks = -(-n_assign // MOE_BLOCK) + N_EXPERTS
    n_rows = n_blocks * MOE_BLOCK
    row_tok = jnp.zeros((n_rows,), jnp.int32).at[dest].set(sorted_tok)
    row_w = jnp.zeros((n_rows,), jnp.float32).at[dest].set(sorted_w)
    block_e = jnp.minimum(jnp.searchsorted(pend, jnp.arange(n_blocks) * MOE_BLOCK, side='right'),
                          N_EXPERTS - 1)

    def block_step(acc, blk):
        e, tok, wts = blk
        xb = xt[tok]
        hcat = xb @ w1[e] + b1[e]
        glu, lin = hcat[:, :D_FF], hcat[:, D_FF:]
        glu = jnp.minimum(glu, SWIGLU_LIMIT)
        lin = jnp.clip(lin, -SWIGLU_LIMIT, SWIGLU_LIMIT)
        hid = glu * jax.nn.sigmoid(SWIGLU_ALPHA * glu) * (lin + 1.0)
        yb = (hid @ w2[e] + b2[e]).astype(jnp.float32)
        acc = acc.at[tok].add(yb * wts[:, None])
        return acc, None

    acc0 = jnp.zeros((t_, d_), jnp.float32)
    acc, _ = lax.scan(block_step, acc0,
                      (block_e, row_tok.reshape(n_blocks, MOE_BLOCK), row_w.reshape(n_blocks, MOE_BLOCK)))
    return acc.reshape(b_, s_, d_).astype(xn.dtype)


def setup_inputs(seed: int = 0) -> dict:
    key = jax.random.key(seed)
    ks = iter(jax.random.split(key, 40))
    f32 = jnp.float32
    L = DEPTH

    def nrm(shape, scale):
        return scale * jax.random.normal(next(ks), shape, f32)

    def gain(shape):
        return 1.0 + 0.02 * jax.random.normal(next(ks), shape, f32)

    inp = {}
    inp["x"] = nrm((BATCH, SEQ, D_MODEL), 1.0)
    inp["mem"] = nrm((BATCH, MEM_TOKENS, D_MODEL), 1.0)
    inp["norm_mix"] = gain((L, D_MODEL))
    inp["norm_ffn"] = gain((L, D_MODEL))
    inp["norm_mem"] = gain((L, D_MODEL))
    inp["w_in"] = nrm((L, D_MODEL, IN_COLS), D_MODEL ** -0.5)
    inp["b_gate"] = nrm((L, N_BRANCHES * D_MODEL), 0.02)
    inp["b_forget"] = jnp.linspace(1.0, 6.0, ATTN_HEADS, dtype=f32)[None, :] + nrm((L, ATTN_HEADS), 0.1)
    inp["q_norm_attn"] = gain((L, ATTN_HEAD_DIM))
    inp["k_norm_attn"] = gain((L, ATTN_HEAD_DIM))
    inp["q_norm_mem"] = gain((L, MEM_HEAD_DIM))
    inp["k_norm_mem"] = gain((L, MEM_HEAD_DIM))
    inp["w_mem_kv"] = nrm((L, D_MODEL, 2 * MEM_WIDTH), D_MODEL ** -0.5)
    inp["lam_re"] = -0.5 + nrm((L, SSM_GROUPS, SSM_STATE), 0.01)
    inp["lam_im"] = math.pi * jnp.arange(SSM_STATE, dtype=f32) + nrm((L, SSM_GROUPS, SSM_STATE), 0.01)
    inp["log_step"] = jax.random.uniform(next(ks), (L, SSM_GROUPS), f32, math.log(1e-3), math.log(1e-1))
    inp["ssm_b_re"] = nrm((L, SSM_GROUPS, SSM_STATE, SSM_GROUP), (2.0 * SSM_GROUP) ** -0.5)
    inp["ssm_b_im"] = nrm((L, SSM_GROUPS, SSM_STATE, SSM_GROUP), (2.0 * SSM_GROUP) ** -0.5)
    inp["ssm_c_re"] = nrm((L, SSM_GROUPS, SSM_GROUP, SSM_STATE), SSM_STATE ** -0.5)
    inp["ssm_c_im"] = nrm((L, SSM_GROUPS, SSM_GROUP, SSM_STATE), SSM_STATE ** -0.5)
    inp["ssm_d"] = nrm((L, SSM_WIDTH), 1.0)
    inp["w_glu"] = nrm((L, SSM_WIDTH, SSM_WIDTH), SSM_WIDTH ** -0.5)
    inp["b_glu"] = nrm((L, SSM_WIDTH), 0.01)
    inp["w_up_attn"] = nrm((L, ATTN_WIDTH, D_MODEL), ATTN_WIDTH ** -0.5)
    inp["w_up_ssm"] = nrm((L, SSM_WIDTH, D_MODEL), SSM_WIDTH ** -0.5)
    inp["w_up_mem"] = nrm((L, MEM_WIDTH, D_MODEL), MEM_WIDTH ** -0.5)
    inp["w_out"] = nrm((L, D_MODEL, D_MODEL), D_MODEL ** -0.5)
    inp["w_router"] = nrm((L, D_MODEL, N_EXPERTS), D_MODEL ** -0.5)
    inp["b_router"] = nrm((L, N_EXPERTS), 0.01)
    inp["exp_w1"] = nrm((L, N_EXPERTS, D_MODEL, 2 * D_FF), D_MODEL ** -0.5)
    inp["exp_b1"] = nrm((L, N_EXPERTS, 2 * D_FF), 0.01)
    inp["exp_w2"] = nrm((L, N_EXPERTS, D_FF, D_MODEL), D_FF ** -0.5)
    inp["exp_b2"] = nrm((L, N_EXPERTS, D_MODEL), 0.01)
    return inp


def reference(x, mem, norm_mix, norm_ffn, norm_mem, w_in, b_gate, b_forget, q_norm_attn, k_norm_attn,
              q_norm_mem, k_norm_mem, w_mem_kv, lam_re, lam_im, log_step, ssm_b_re, ssm_b_im, ssm_c_re,
              ssm_c_im, ssm_d, w_glu, b_glu, w_up_attn, w_up_ssm, w_up_mem, w_out, w_router, b_router,
              exp_w1, exp_b1, exp_w2, exp_b2):
    b_, s_, _ = x.shape
    split_idx = [int(v) for v in np.cumsum(IN_SPLITS)[:-1]]
    for layer in range(DEPTH):
        h = rmsnorm(x, norm_mix[layer])
        proj = h @ w_in[layer]
        q_a, k_a, v_a, f_logit, u_s, q_m, g_logit = jnp.split(proj, split_idx, axis=-1)

        q_a = rmsnorm(q_a.reshape(b_, s_, ATTN_HEADS, ATTN_HEAD_DIM), q_norm_attn[layer])
        k_a = rmsnorm(k_a.reshape(b_, s_, ATTN_HEADS, ATTN_HEAD_DIM), k_norm_attn[layer])
        v_a = v_a.reshape(b_, s_, ATTN_HEADS, ATTN_HEAD_DIM)
        logf = jax.nn.log_sigmoid(f_logit.astype(jnp.float32) + b_forget[layer].astype(jnp.float32))
        y_attn = fox_attention(q_a, k_a, v_a, logf)

        y_ssm = s5_mixer(u_s, lam_re[layer], lam_im[layer], log_step[layer], ssm_b_re[layer],
                         ssm_b_im[layer], ssm_c_re[layer], ssm_c_im[layer], ssm_d[layer],
                         w_glu[layer], b_glu[layer])

        mem_h = rmsnorm(mem, norm_mem[layer])
        k_m, v_m = jnp.split(mem_h @ w_mem_kv[layer], 2, axis=-1)
        k_m = rmsnorm(k_m.reshape(b_, MEM_TOKENS, MEM_HEADS, MEM_HEAD_DIM), k_norm_mem[layer])
        v_m = v_m.reshape(b_, MEM_TOKENS, MEM_HEADS, MEM_HEAD_DIM)
        q_m = rmsnorm(q_m.reshape(b_, s_, MEM_HEADS, MEM_HEAD_DIM), q_norm_mem[layer])
        y_mem = mem_attention(q_m, k_m, v_m)

        g_a, g_s, g_m = jnp.split(jax.nn.sigmoid(g_logit + b_gate[layer]), N_BRANCHES, axis=-1)
        merged = (g_a * (y_attn @ w_up_attn[layer]) + g_s * (y_ssm @ w_up_ssm[layer])
                  + g_m * (y_mem @ w_up_mem[layer]))
        x = x + merged @ w_out[layer]

        x = x + moe_ffn(rmsnorm(x, norm_ffn[layer]), w_router[layer], b_router[layer],
                        exp_w1[layer], exp_b1[layer], exp_w2[layer], exp_b2[layer])
    return x
```

```python
import functools
import math

import jax
import jax.numpy as jnp
from jax import lax
from jax.experimental import pallas as pl
from jax.experimental.pallas import tpu as pltpu

F32 = jnp.float32
BF16 = jnp.bfloat16
I32 = jnp.int32
U32 = jnp.uint32

RMS_EPS = 1e-6
TOP_K = 4
SWIGLU_ALPHA = 1.702
SWIGLU_LIMIT = 7.0

LANES = 128
VMEM_LIMIT = 56 * 1024 * 1024
SSM_CHUNK = 8
HIGHEST = lax.Precision.HIGHEST


def _pick_block(n, target, align=LANES):
    if n <= target:
        return n
    b = target // align * align
    while n % b:
        b -= align
    return b


def _params(*sem):
    return pltpu.CompilerParams(dimension_semantics=sem, vmem_limit_bytes=VMEM_LIMIT)


def _rmsnorm_kernel(x_ref, g_ref, o_ref):
    x = x_ref[...]
    ms = jnp.mean(x * x, axis=-1, keepdims=True)
    o_ref[...] = (x * lax.rsqrt(ms + RMS_EPS) * g_ref[...]).astype(o_ref.dtype)


def _rmsnorm(x, g, out_dtype, bm=256):
    m, d = x.shape
    bm = min(bm, m)
    return pl.pallas_call(
        _rmsnorm_kernel,
        grid=(m // bm,),
        in_specs=[pl.BlockSpec((bm, d), lambda i: (i, 0)), pl.BlockSpec((1, d), lambda i: (0, 0))],
        out_specs=pl.BlockSpec((bm, d), lambda i: (i, 0)),
        out_shape=jax.ShapeDtypeStruct((m, d), out_dtype),
        compiler_params=_params("parallel"),
        name="rmsnorm",
    )(x, g.reshape(1, d))


def _mm_kernel(*refs, n_extra, epilogue):
    x_ref, w_ref = refs[0], refs[1]
    extras = refs[2:2 + n_extra]
    o_ref = refs[2 + n_extra]
    acc = jnp.dot(x_ref[...].astype(BF16), w_ref[...], preferred_element_type=F32)
    o_ref[...] = epilogue(acc, *[e[...] for e in extras]).astype(o_ref.dtype)


def _matmul(x, w, extras, epilogue, out_dtype, bm, bn, name):
    m, k = x.shape
    n = w.shape[1]
    bm, bn = _pick_block(m, bm, 8), _pick_block(n, bn)
    in_specs = [pl.BlockSpec((bm, k), lambda i, j: (i, 0)), pl.BlockSpec((k, bn), lambda i, j: (0, j))]
    args = [x, w]
    for arr, kind, off in extras:
        if kind == "row":
            in_specs.append(pl.BlockSpec((1, bn), lambda i, j, off=off: (0, j + off)))
        else:
            in_specs.append(pl.BlockSpec((bm, bn), lambda i, j, off=off: (i, j + off)))
        args.append(arr)
    return pl.pallas_call(
        functools.partial(_mm_kernel, n_extra=len(extras), epilogue=epilogue),
        grid=(m // bm, n // bn),
        in_specs=in_specs,
        out_specs=pl.BlockSpec((bm, bn), lambda i, j: (i, j)),
        out_shape=jax.ShapeDtypeStruct((m, n), out_dtype),
        compiler_params=_params("parallel", "parallel"),
        name=name,
    )(*args)


def _ep_plain(acc):
    return acc


def _ep_headnorm(acc, gain, *, dh, scale):
    outs = []
    for h in range(acc.shape[1] // dh):
        blk = acc[:, h * dh:(h + 1) * dh]
        ms = jnp.mean(blk * blk, axis=-1, keepdims=True)
        outs.append(blk * lax.rsqrt(ms + RMS_EPS) * gain[:, h * dh:(h + 1) * dh] * scale)
    return outs[0] if len(outs) == 1 else jnp.concatenate(outs, axis=1)


def _ep_sigmoid_bias(acc, b):
    return jax.nn.sigmoid(acc + b)


def _ep_glu(acc, b, y):
    return y * jax.nn.sigmoid(acc + b)


def _ep_residual(acc, r):
    return acc + r


def _forget_cumsum_kernel(f_ref, b_ref, c_ref, carry_ref, *, bs):
    @pl.when(pl.program_id(0) == 0)
    def _():
        carry_ref[...] = jnp.zeros_like(carry_ref)

    z = f_ref[...] + b_ref[...]
    logf = jnp.minimum(z, 0.0) - jnp.log1p(jnp.exp(-jnp.abs(z)))
    row = lax.broadcasted_iota(I32, (bs, bs), 0)
    col = lax.broadcasted_iota(I32, (bs, bs), 1)
    tri = jnp.where(col <= row, 1.0, 0.0).astype(BF16)
    hi = logf.astype(BF16)
    r1 = logf - hi.astype(F32)
    mid = r1.astype(BF16)
    lo = (r1 - mid.astype(F32)).astype(BF16)
    cs = (jnp.dot(tri, hi, preferred_element_type=F32) + jnp.dot(tri, mid, preferred_element_type=F32)
          + jnp.dot(tri, lo, preferred_element_type=F32)) + carry_ref[...]
    c_ref[...] = cs
    carry_ref[...] = cs[bs - 1:bs, :]


def _forget_cumsum(f_logit, b_forget_pad, bs=256):
    s, w = f_logit.shape
    bs = min(bs, s)
    return pl.pallas_call(
        functools.partial(_forget_cumsum_kernel, bs=bs),
        grid=(s // bs,),
        in_specs=[pl.BlockSpec((bs, w), lambda i: (i, 0)), pl.BlockSpec((1, w), lambda i: (0, 0))],
        out_specs=pl.BlockSpec((bs, w), lambda i: (i, 0)),
        out_shape=jax.ShapeDtypeStruct((s, w), F32),
        scratch_shapes=[pltpu.VMEM((1, w), F32)],
        compiler_params=_params("arbitrary"),
        name="forget_cumsum",
    )(f_logit, b_forget_pad)


def _fox_kernel(q_ref, k_ref, v_ref, ccol_ref, crow_ref, o_ref, m_ref, l_ref, acc_ref, *, blk):
    i = pl.program_id(1)
    q = q_ref[...]
    ci = ccol_ref[...]
    m_ref[...] = jnp.full_like(m_ref, -jnp.inf)
    l_ref[...] = jnp.zeros_like(l_ref)
    acc_ref[...] = jnp.zeros_like(acc_ref)

    def step(j, masked):
        start = pl.multiple_of(j * blk, blk)
        kj = k_ref[pl.ds(start, blk), :]
        vj = v_ref[pl.ds(start, blk), :]
        s = lax.dot_general(q, kj, (((1,), (1,)), ((), ())), preferred_element_type=F32)
        s = s + (ci - crow_ref[j])
        if masked:
            r = lax.broadcasted_iota(I32, s.shape, 0)
            c = lax.broadcasted_iota(I32, s.shape, 1)
            s = jnp.where(c <= r, s, -jnp.inf)
        m_old = m_ref[...]
        m_new = jnp.maximum(m_old, jnp.max(s, axis=-1, keepdims=True))
        p = jnp.exp(s - m_new)
        corr = jnp.exp(m_old - m_new)
        l_ref[...] = l_ref[...] * corr + jnp.sum(p, axis=-1, keepdims=True)
        acc_ref[...] = acc_ref[...] * corr + jnp.dot(p.astype(BF16), vj, preferred_element_type=F32)
        m_ref[...] = m_new

    def body(j, carry):
        step(j, False)
        return carry

    lax.fori_loop(0, i, body, 0)
    step(i, True)
    o_ref[...] = (acc_ref[...] / l_ref[...]).astype(o_ref.dtype)


def _fox_attention(q, k, v, c, n_heads, dh, blk=512):
    s = q.shape[0]
    blk = min(blk, s)
    nb = s // blk
    c_t = jnp.transpose(c[:, :n_heads])
    c_col = c_t.reshape(n_heads, s, 1)
    c_row = c_t.reshape(n_heads, nb, 1, blk)
    return pl.pallas_call(
        functools.partial(_fox_kernel, blk=blk),
        grid=(n_heads, nb),
        in_specs=[
            pl.BlockSpec((blk, dh), lambda h, i: (i, h)),
            pl.BlockSpec((s, dh), lambda h, i: (0, h)),
            pl.BlockSpec((s, dh), lambda h, i: (0, h)),
            pl.BlockSpec((None, blk, 1), lambda h, i: (h, i, 0)),
            pl.BlockSpec((None, nb, 1, blk), lambda h, i: (h, 0, 0, 0)),
        ],
        out_specs=pl.BlockSpec((blk, dh), lambda h, i: (i, h)),
        out_shape=jax.ShapeDtypeStruct((s, n_heads * dh), BF16),
        scratch_shapes=[pltpu.VMEM((blk, 1), F32), pltpu.VMEM((blk, 1), F32), pltpu.VMEM((blk, dh), F32)],
        compiler_params=_params("parallel", "arbitrary"),
        name="fox_attention",
    )(q, k, v, c_col, c_row)


def _s5_params_kernel(lre_ref, lim_ref, ls_ref, bre_ref, bim_ref, ctre_ref, ctim_ref, cre_ref, cim_ref,
                      zre_ref, zim_ref, wre_ref, wim_ref, m_ref, alre_ref, alim_ref, *, chunk, n_ch):
    lre, lim = lre_ref[...], lim_ref[...]
    step = jnp.exp(ls_ref[...])
    ar, ai = lre * step, lim * step
    lc = chunk * n_ch
    tau = (lax.broadcasted_iota(I32, (1, 1, lc), 2) // n_ch).astype(F32)

    def cexp(t):
        mag = jnp.exp(ar * t)
        return mag * jnp.cos(ai * t), mag * jnp.sin(ai * t)

    e1r, e1i = cexp(1.0)
    nr, ni = e1r - 1.0, e1i
    den = lre * lre + lim * lim
    cr, ci = (nr * lre + ni * lim) / den, (ni * lre - nr * lim) / den
    bre, bim = bre_ref[...], bim_ref[...]
    bbr, bbi = cr * bre - ci * bim, cr * bim + ci * bre
    er, ei = cexp(tau)
    zr, zi = er * bbr - ei * bbi, er * bbi + ei * bbr
    zre_ref[...] = zr
    zim_ref[...] = zi
    e1r_t, e1i_t = cexp(tau + 1.0)
    ctre, ctim = ctre_ref[...], ctim_ref[...]
    wre_ref[...] = ctre * e1r_t - ctim * e1i_t
    wim_ref[...] = ctre * e1i_t + ctim * e1r_t
    dn = (((2,), (1,)), ((0,), (0,)))
    m_ref[...] = (lax.dot_general(cre_ref[...], zr, dn, precision=HIGHEST, preferred_element_type=F32)
                  - lax.dot_general(cim_ref[...], zi, dn, precision=HIGHEST, preferred_element_type=F32))
    alr, ali = cexp(float(chunk))
    alre_ref[...] = alr
    alim_ref[...] = ali


def _s5_operators(lam_re, lam_im, log_step, b_re, b_im, c_re, c_im, chunk):
    g, n = lam_re.shape
    n_ch = b_re.shape[-1]
    gs = LANES // n_ch
    assert g % gs == 0
    nsl, lc = g // gs, chunk * n_ch
    tile = lambda a: jnp.tile(a, (1, 1, chunk))
    ct_re, ct_im = jnp.transpose(c_re, (0, 2, 1)), jnp.transpose(c_im, (0, 2, 1))
    spec3 = lambda a, b: pl.BlockSpec((gs, a, b), lambda i: (i, 0, 0))
    zre, zim, wre, wim, mk, alre, alim = pl.pallas_call(
        functools.partial(_s5_params_kernel, chunk=chunk, n_ch=n_ch),
        grid=(nsl,),
        in_specs=[spec3(n, 1), spec3(n, 1), spec3(1, 1), spec3(n, lc), spec3(n, lc), spec3(n, lc), spec3(n, lc),
                  spec3(n_ch, n), spec3(n_ch, n)],
        out_specs=[spec3(n, lc), spec3(n, lc), spec3(n, lc), spec3(n, lc), spec3(n_ch, lc), spec3(n, 1), spec3(n, 1)],
        out_shape=[jax.ShapeDtypeStruct((g, n, lc), F32)] * 4 + [jax.ShapeDtypeStruct((g, n_ch, lc), F32)]
        + [jax.ShapeDtypeStruct((g, n, 1), F32)] * 2,
        compiler_params=_params("parallel"),
        name="s5_params",
    )(lam_re.reshape(g, n, 1), lam_im.reshape(g, n, 1), log_step.reshape(g, 1, 1), tile(b_re), tile(b_im),
      tile(ct_re), tile(ct_im), c_re, c_im)

    eye = jnp.eye(gs, dtype=F32)
    m5 = mk.reshape(nsl, gs, n_ch, chunk, n_ch)
    kk = jnp.transpose(m5, (0, 3, 1, 4, 2))
    kk = (kk[:, :, :, :, None, :] * eye[None, None, :, None, :, None]).reshape(nsl, chunk, LANES, LANES)
    d = jnp.arange(chunk)[None, :] - jnp.arange(chunk)[:, None]
    tm = jnp.where((d >= 0)[None, :, :, None, None], kk[:, jnp.clip(d, 0, chunk - 1)], 0.0)
    t_op = jnp.transpose(tm, (0, 1, 3, 2, 4)).reshape(nsl, chunk * LANES, chunk * LANES).astype(BF16)

    def inject(z):
        z5 = z.reshape(nsl, gs, n, chunk, n_ch)[:, :, :, ::-1, :]
        pd = jnp.transpose(z5, (0, 3, 1, 4, 2))
        return (pd[:, :, :, :, None, :] * eye[None, None, :, None, :, None]).reshape(nsl, chunk * LANES, gs * n)

    def readout(w):
        w5 = w.reshape(nsl, gs, n, chunk, n_ch)
        return (w5[:, :, :, :, None, :] * eye[None, :, None, None, :, None]).reshape(nsl, gs * n, chunk * LANES)

    p_op = jnp.concatenate([inject(zre), inject(zim)], axis=-1).astype(BF16)
    q_op = jnp.concatenate([readout(wre), -readout(wim)], axis=1).astype(BF16)
    a_l = jnp.concatenate([alre.reshape(nsl, 1, gs * n), alim.reshape(nsl, 1, gs * n)], axis=-1)
    return t_op, p_op, q_op, a_l


def _gelu_tanh(x):
    return 0.5 * x * (1.0 + jnp.tanh(math.sqrt(2.0 / math.pi) * (x + 0.044715 * (x * x * x))))


def _s5_conv_kernel(u_ref, t_ref, p_ref, q_ref, a_ref, d_ref, o_ref, hst_ref, s_ref, hp_ref, *, chunk, bk):
    hn = a_ref.shape[1] // 2

    @pl.when(pl.program_id(1) == 0)
    def _():
        hst_ref[...] = jnp.zeros_like(hst_ref)

    x = jnp.concatenate([u_ref[pl.ds(s, bk, stride=chunk), :] for s in range(chunk)], axis=1).astype(BF16)
    s_ref[...] = jnp.dot(x, p_ref[...], preferred_element_type=F32)
    ar, ai = a_ref[:, :hn], a_ref[:, hn:]

    def body(k, h):
        hr, hi = h
        hp_ref[pl.ds(k, 1), :hn] = hr
        hp_ref[pl.ds(k, 1), hn:] = hi
        row = s_ref[pl.ds(k, 1), :]
        return ar * hr - ai * hi + row[:, :hn], ar * hi + ai * hr + row[:, hn:]

    hr, hi = lax.fori_loop(0, bk, body, (hst_ref[:, :hn], hst_ref[:, hn:]))
    hst_ref[:, :hn] = hr
    hst_ref[:, hn:] = hi

    y = (jnp.dot(x, t_ref[...], preferred_element_type=F32)
         + jnp.dot(hp_ref[...].astype(BF16), q_ref[...], preferred_element_type=F32))
    d = d_ref[...]
    for t in range(chunk):
        yt = y[:, t * LANES:(t + 1) * LANES] + d * u_ref[pl.ds(t, bk, stride=chunk), :]
        o_ref[pl.ds(t, bk, stride=chunk), :] = _gelu_tanh(yt)


def _s5_conv(u, t_op, p_op, q_op, a_l, d_skip, chunk, bk=512):
    s, p = u.shape
    nsl = p // LANES
    bk = min(bk, s // chunk)
    bm = bk * chunk
    hn2 = a_l.shape[-1]
    return pl.pallas_call(
        functools.partial(_s5_conv_kernel, chunk=chunk, bk=bk),
        grid=(nsl, s // bm),
        in_specs=[
            pl.BlockSpec((bm, LANES), lambda sl, i: (i, sl)),
            pl.BlockSpec((None, chunk * LANES, chunk * LANES), lambda sl, i: (sl, 0, 0)),
            pl.BlockSpec((None, chunk * LANES, hn2), lambda sl, i: (sl, 0, 0)),
            pl.BlockSpec((None, hn2, chunk * LANES), lambda sl, i: (sl, 0, 0)),
            pl.BlockSpec((None, 1, hn2), lambda sl, i: (sl, 0, 0)),
            pl.BlockSpec((1, LANES), lambda sl, i: (0, sl)),
        ],
        out_specs=pl.BlockSpec((bm, LANES), lambda sl, i: (i, sl)),
        out_shape=jax.ShapeDtypeStruct((s, p), F32),
        scratch_shapes=[pltpu.VMEM((1, hn2), F32), pltpu.VMEM((bk, hn2), F32), pltpu.VMEM((bk, hn2), F32)],
        compiler_params=_params("parallel", "arbitrary"),
        name="s5_conv",
    )(u, t_op, p_op, q_op, a_l, d_skip.reshape(1, p))


def _mem_attn_kernel(q_ref, k_ref, v_ref, o_ref, *, n_heads, dm):
    for h in range(n_heads):
        sl = slice(h * dm, (h + 1) * dm)
        s = lax.dot_general(q_ref[:, sl], k_ref[:, sl], (((1,), (1,)), ((), ())), preferred_element_type=F32)
        p = jnp.exp(s - jnp.max(s, axis=-1, keepdims=True))
        o = jnp.dot(p.astype(BF16), v_ref[:, sl], preferred_element_type=F32) / jnp.sum(p, axis=-1, keepdims=True)
        o_ref[:, sl] = o.astype(o_ref.dtype)


def _mem_attention(q, k, v, n_heads, dm, bm=512):
    s, w = q.shape
    m = k.shape[0]
    bm = min(bm, s)
    return pl.pallas_call(
        functools.partial(_mem_attn_kernel, n_heads=n_heads, dm=dm),
        grid=(s // bm,),
        in_specs=[pl.BlockSpec((bm, w), lambda i: (i, 0)), pl.BlockSpec((m, w), lambda i: (0, 0)),
                  pl.BlockSpec((m, w), lambda i: (0, 0))],
        out_specs=pl.BlockSpec((bm, w), lambda i: (i, 0)),
        out_shape=jax.ShapeDtypeStruct((s, w), BF16),
        compiler_params=_params("parallel"),
        name="mem_attention",
    )(q, k, v)


def _merge_kernel(ya_ref, ys_ref, ym_ref, wa_ref, ws_ref, wm_ref, ga_ref, gs_ref, gm_ref, o_ref):
    a = jnp.dot(ya_ref[...], wa_ref[...], preferred_element_type=F32)
    s = jnp.dot(ys_ref[...], ws_ref[...], preferred_element_type=F32)
    m = jnp.dot(ym_ref[...], wm_ref[...], preferred_element_type=F32)
    o_ref[...] = (ga_ref[...].astype(F32) * a + gs_ref[...].astype(F32) * s
                  + gm_ref[...].astype(F32) * m).astype(o_ref.dtype)


def _merge(ya, ys, ym, wa, ws, wm, gates, bm=1024, bn=512):
    s, d = ya.shape[0], wa.shape[1]
    bm, bn = min(bm, s), min(bn, d)
    nb = d // bn
    row = lambda a: pl.BlockSpec((bm, a.shape[1]), lambda i, j: (i, 0))
    col = lambda a: pl.BlockSpec((a.shape[0], bn), lambda i, j: (0, j))
    gate = lambda b: pl.BlockSpec((bm, bn), lambda i, j, b=b: (i, j + b * nb))
    return pl.pallas_call(
        _merge_kernel,
        grid=(s // bm, nb),
        in_specs=[row(ya), row(ys), row(ym), col(wa), col(ws), col(wm), gate(0), gate(1), gate(2)],
        out_specs=pl.BlockSpec((bm, bn), lambda i, j: (i, j)),
        out_shape=jax.ShapeDtypeStruct((s, d), BF16),
        compiler_params=_params("parallel", "parallel"),
        name="merge",
    )(ya, ys, ym, wa, ws, wm, gates, gates, gates)


def _router_kernel(x_ref, g_ref, wr_ref, br_ref, xp_ref, idx_ref, gate_ref, rank_ref, cnt_ref, carry_ref, *, bm):
    @pl.when(pl.program_id(0) == 0)
    def _():
        carry_ref[...] = jnp.zeros_like(carry_ref)

    x = x_ref[...]
    ms = jnp.mean(x * x, axis=-1, keepdims=True)
    xn = x * lax.rsqrt(ms + RMS_EPS) * g_ref[...]
    d2 = xn.shape[1] // 2
    lo = pltpu.bitcast(xn[:, :d2].astype(BF16).astype(F32), U32) >> 16
    hi = pltpu.bitcast(xn[:, d2:].astype(BF16).astype(F32), U32) & jnp.uint32(0xFFFF0000)
    xp_ref[...] = lo | hi

    n_exp = wr_ref.shape[1]
    logits = jnp.dot(xn, wr_ref[...], precision=HIGHEST, preferred_element_type=F32) + br_ref[...]
    lane = lax.broadcasted_iota(I32, logits.shape, 1)
    kcol = lax.broadcasted_iota(I32, (bm, TOP_K), 1)
    work = logits
    member = jnp.zeros_like(logits)
    idxs, vals = [], []
    for _ in range(TOP_K):
        mx = jnp.max(work, axis=-1, keepdims=True)
        ik = jnp.min(jnp.where(work == mx, lane, n_exp), axis=-1, keepdims=True)
        sel = lane == ik
        member = jnp.where(sel, 1.0, member)
        work = jnp.where(sel, -jnp.inf, work)
        idxs.append(ik)
        vals.append(mx)
    es = [jnp.exp(v - vals[0]) for v in vals]
    den = es[0] + es[1] + es[2] + es[3]

    r = lax.broadcasted_iota(I32, (bm, bm), 0)
    c = lax.broadcasted_iota(I32, (bm, bm), 1)
    tri = jnp.where(c < r, 1.0, 0.0).astype(BF16)
    prefix = jnp.dot(tri, member.astype(BF16), preferred_element_type=F32) + carry_ref[...]
    total = carry_ref[...] + jnp.sum(member, axis=0, keepdims=True)
    carry_ref[...] = total
    cnt_ref[...] = total

    idx_o = jnp.zeros((bm, TOP_K), I32)
    gate_o = jnp.zeros((bm, TOP_K), F32)
    rank_o = jnp.zeros((bm, TOP_K), F32)
    for k in range(TOP_K):
        rk = jnp.sum(jnp.where(lane == idxs[k], prefix, 0.0), axis=-1, keepdims=True)
        idx_o = jnp.where(kcol == k, idxs[k], idx_o)
        gate_o = jnp.where(kcol == k, es[k] / den, gate_o)
        rank_o = jnp.where(kcol == k, rk, rank_o)
    idx_ref[...] = idx_o
    gate_ref[...] = gate_o
    rank_ref[...] = rank_o.astype(I32)


def _router(x1, g, w_router, b_router, bm=256):
    t, d = x1.shape
    n_exp = w_router.shape[1]
    bm = min(bm, t)
    small = lambda: pl.BlockSpec((bm, TOP_K), lambda i: (i, 0))
    return pl.pallas_call(
        functools.partial(_router_kernel, bm=bm),
        grid=(t // bm,),
        in_specs=[pl.BlockSpec((bm, d), lambda i: (i, 0)), pl.BlockSpec((1, d), lambda i: (0, 0)),
                  pl.BlockSpec((d, n_exp), lambda i: (0, 0)), pl.BlockSpec((1, n_exp), lambda i: (0, 0))],
        out_specs=[pl.BlockSpec((bm, d // 2), lambda i: (i, 0)), small(), small(), small(),
                   pl.BlockSpec((1, n_exp), lambda i: (0, 0))],
        out_shape=[jax.ShapeDtypeStruct((t, d // 2), U32), jax.ShapeDtypeStruct((t, TOP_K), I32),
                   jax.ShapeDtypeStruct((t, TOP_K), F32), jax.ShapeDtypeStruct((t, TOP_K), I32),
                   jax.ShapeDtypeStruct((1, n_exp), F32)],
        scratch_shapes=[pltpu.VMEM((1, n_exp), F32)],
        compiler_params=_params("arbitrary"),
        name="router",
    )(x1, g.reshape(1, d), w_router, b_router.reshape(1, n_exp))


def _dispatch_kernel(fill_ref, dest_ref, xp_ref, xs_ref, zero_ref, sem_z, sem_r, *, bm, bm_e, n_tiles):
    @pl.when(pl.program_id(0) == 0)
    def _():
        zero_ref[...] = jnp.zeros_like(zero_ref)

        def fill(t, wait):
            @pl.when(fill_ref[t] > 0)
            def _():
                cp = pltpu.make_async_copy(zero_ref, xs_ref.at[pl.ds(t * bm_e, bm_e), :], sem_z)
                cp.wait() if wait else cp.start()

        lax.fori_loop(0, n_tiles, lambda t, c: (fill(t, False), c)[1], 0)
        lax.fori_loop(0, n_tiles, lambda t, c: (fill(t, True), c)[1], 0)

    def row_copy(r, k):
        return pltpu.make_async_copy(xp_ref.at[pl.ds(r, 1), :], xs_ref.at[pl.ds(dest_ref[r * TOP_K + k], 1), :], sem_r)

    def issue(r, c):
        for k in range(TOP_K):
            row_copy(r, k).start()
        return c

    def drain(r, c):
        for k in range(TOP_K):
            row_copy(r, k).wait()
        return c

    lax.fori_loop(0, bm, issue, 0)
    lax.fori_loop(0, bm, drain, 0)


def _dispatch(xp, dest_flat, tile_fill, bm_e, bm=256):
    t, d2 = xp.shape
    bm = min(bm, t)
    n_tiles = tile_fill.shape[0]
    n_rows = n_tiles * bm_e
    return pl.pallas_call(
        functools.partial(_dispatch_kernel, bm=bm, bm_e=bm_e, n_tiles=n_tiles),
        grid_spec=pltpu.PrefetchScalarGridSpec(
            num_scalar_prefetch=1,
            grid=(t // bm,),
            in_specs=[pl.BlockSpec((bm * TOP_K,), lambda i, lt: (i,), memory_space=pltpu.SMEM),
                      pl.BlockSpec((bm, d2), lambda i, lt: (i, 0))],
            out_specs=pl.BlockSpec(memory_space=pl.ANY),
            scratch_shapes=[pltpu.VMEM((bm_e, d2), U32), pltpu.SemaphoreType.DMA(()), pltpu.SemaphoreType.DMA(())],
        ),
        out_shape=jax.ShapeDtypeStruct((n_rows, d2), U32),
        compiler_params=_params("arbitrary"),
        name="moe_dispatch",
    )(tile_fill, dest_flat, xp)


def _moe_kernel(te_ref, nu_ref, xs_ref, w1g_ref, w1l_ref, b1g_ref, b1l_ref, w2_ref, b2_ref, o_ref, xlo_ref, xhi_ref):
    t, f = pl.program_id(0), pl.program_id(1)

    @pl.when(t < nu_ref[0])
    def _():
        @pl.when(f == 0)
        def _():
            w = xs_ref[...]
            xlo_ref[...] = pltpu.bitcast(w << 16, F32).astype(BF16)
            xhi_ref[...] = pltpu.bitcast(w & jnp.uint32(0xFFFF0000), F32).astype(BF16)

        d2 = xlo_ref.shape[1]
        xlo, xhi = xlo_ref[...], xhi_ref[...]
        glu = (jnp.dot(xlo, w1g_ref[:d2, :], preferred_element_type=F32)
               + jnp.dot(xhi, w1g_ref[d2:, :], preferred_element_type=F32) + b1g_ref[...])
        lin = (jnp.dot(xlo, w1l_ref[:d2, :], preferred_element_type=F32)
               + jnp.dot(xhi, w1l_ref[d2:, :], preferred_element_type=F32) + b1l_ref[...])
        glu = jnp.minimum(glu, SWIGLU_LIMIT)
        lin = jnp.clip(lin, -SWIGLU_LIMIT, SWIGLU_LIMIT)
        hid = glu * jax.nn.sigmoid(SWIGLU_ALPHA * glu) * (lin + 1.0)
        y = jnp.dot(hid.astype(BF16), w2_ref[...], preferred_element_type=F32)

        @pl.when(f == 0)
        def _():
            o_ref[...] = y + b2_ref[...]

        @pl.when(f > 0)
        def _():
            o_ref[...] += y

    @pl.when(jnp.logical_and(t >= nu_ref[0], f == 0))
    def _():
        o_ref[...] = jnp.zeros_like(o_ref)


def _moe_ffn(xs, tile_expert, n_used, w1, b1, w2, b2, bm_e, tf=256):
    n_rows, d2 = xs.shape
    n_exp, d, f2 = w1.shape
    ff = f2 // 2
    tf = _pick_block(ff, tf)
    nf = ff // tf
    n_tiles = n_rows // bm_e
    row = lambda t, f, te, nu: (jnp.minimum(t, nu[0] - 1), 0)
    return pl.pallas_call(
        _moe_kernel,
        grid_spec=pltpu.PrefetchScalarGridSpec(
            num_scalar_prefetch=2,
            grid=(n_tiles, nf),
            in_specs=[
                pl.BlockSpec((bm_e, d2), row),
                pl.BlockSpec((None, d, tf), lambda t, f, te, nu: (te[t], 0, f)),
                pl.BlockSpec((None, d, tf), lambda t, f, te, nu: (te[t], 0, f + nf)),
                pl.BlockSpec((None, 1, tf), lambda t, f, te, nu: (te[t], 0, f)),
                pl.BlockSpec((None, 1, tf), lambda t, f, te, nu: (te[t], 0, f + nf)),
                pl.BlockSpec((None, tf, d), lambda t, f, te, nu: (te[t], f, 0)),
                pl.BlockSpec((None, 1, d), lambda t, f, te, nu: (te[t], 0, 0)),
            ],
            out_specs=pl.BlockSpec((bm_e, d), lambda t, f, te, nu: (t, 0)),
            scratch_shapes=[pltpu.VMEM((bm_e, d2), BF16), pltpu.VMEM((bm_e, d2), BF16)],
        ),
        out_shape=jax.ShapeDtypeStruct((n_rows, d), F32),
        compiler_params=_params("arbitrary", "arbitrary"),
        name="moe_ffn",
    )(tile_expert, n_used, xs, w1, w1, b1.reshape(n_exp, 1, f2), b1.reshape(n_exp, 1, f2), w2, b2.reshape(n_exp, 1, d))


def _combine_kernel(dest_ref, x_ref, gate_ref, ys_ref, o_ref, buf_ref, sem, *, bm):
    def row_copy(r, k):
        return pltpu.make_async_copy(ys_ref.at[pl.ds(dest_ref[r * TOP_K + k], 1), :],
                                     buf_ref.at[k, pl.ds(r, 1), :], sem)

    def issue(r, c):
        for k in range(TOP_K):
            row_copy(r, k).start()
        return c

    def drain(r, c):
        for k in range(TOP_K):
            row_copy(r, k).wait()
        return c

    lax.fori_loop(0, bm, issue, 0)
    lax.fori_loop(0, bm, drain, 0)
    acc = x_ref[...]
    g = gate_ref[...]
    for k in range(TOP_K):
        acc = acc + g[:, k:k + 1] * buf_ref[k]
    o_ref[...] = acc


def _combine(x1, gate, dest_flat, ys, bm=128):
    t, d = x1.shape
    bm = min(bm, t)
    return pl.pallas_call(
        functools.partial(_combine_kernel, bm=bm),
        grid=(t // bm,),
        in_specs=[pl.BlockSpec((bm * TOP_K,), lambda i: (i,), memory_space=pltpu.SMEM),
                  pl.BlockSpec((bm, d), lambda i: (i, 0)),
                  pl.BlockSpec((bm, TOP_K), lambda i: (i, 0)),
                  pl.BlockSpec(memory_space=pl.ANY)],
        out_specs=pl.BlockSpec((bm, d), lambda i: (i, 0)),
        out_shape=jax.ShapeDtypeStruct((t, d), F32),
        scratch_shapes=[pltpu.VMEM((TOP_K, bm, d), F32), pltpu.SemaphoreType.DMA(())],
        compiler_params=_params("arbitrary"),
        name="moe_combine",
    )(dest_flat, x1, gate, ys)


def _mixer(x2, mem2, norm_mix, norm_mem, w_in, b_gate, b_forget, q_norm_attn, k_norm_attn, q_norm_mem, k_norm_mem,
           w_mem_kv, lam_re, lam_im, log_step, ssm_b_re, ssm_b_im, ssm_c_re, ssm_c_im, ssm_d, w_glu, b_glu,
           w_up_attn, w_up_ssm, w_up_mem, w_out):
    s, d = x2.shape
    n_heads, dh = b_forget.shape[0], q_norm_attn.shape[0]
    aw = n_heads * dh
    pw = ssm_d.shape[0]
    dm = q_norm_mem.shape[0]
    mw = w_up_mem.shape[0]
    mem_heads = mw // dm
    o_k, o_v, o_f, o_u, o_qm, o_g = aw, 2 * aw, 3 * aw, 3 * aw + n_heads, 3 * aw + n_heads + pw, 3 * aw + n_heads + pw + mw
    bf = lambda a: a.astype(BF16)

    h = _rmsnorm(x2, norm_mix, BF16)
    q = _matmul(h, bf(w_in[:, :o_k]), [(jnp.tile(q_norm_attn, n_heads).reshape(1, aw), "row", 0)],
                functools.partial(_ep_headnorm, dh=dh, scale=dh ** -0.5), BF16, 1024, 512, "in_q")
    k = _matmul(h, bf(w_in[:, o_k:o_v]), [(jnp.tile(k_norm_attn, n_heads).reshape(1, aw), "row", 0)],
                functools.partial(_ep_headnorm, dh=dh, scale=1.0), BF16, 1024, 512, "in_k")
    v = _matmul(h, bf(w_in[:, o_v:o_f]), [], _ep_plain, BF16, 1024, 512, "in_v")
    w_f = jnp.pad(w_in[:, o_f:o_u], ((0, 0), (0, LANES - n_heads)))
    f_logit = _matmul(h, bf(w_f), [], _ep_plain, F32, 1024, LANES, "in_f")
    u = _matmul(h, bf(w_in[:, o_u:o_qm]), [], _ep_plain, F32, 1024, 512, "in_u")
    qm = _matmul(h, bf(w_in[:, o_qm:o_g]), [(jnp.tile(q_norm_mem, mem_heads).reshape(1, mw), "row", 0)],
                 functools.partial(_ep_headnorm, dh=dm, scale=dm ** -0.5), BF16, 1024, 512, "in_qm")
    gates = _matmul(h, bf(w_in[:, o_g:]), [(b_gate.reshape(1, 3 * d), "row", 0)], _ep_sigmoid_bias, BF16,
                    1024, 1024, "in_gates")

    c = _forget_cumsum(f_logit, jnp.pad(b_forget, (0, LANES - n_heads)).reshape(1, LANES))
    y_attn = _fox_attention(q, k, v, c, n_heads, dh)

    t_op, p_op, q_op, a_l = _s5_operators(lam_re, lam_im, log_step, ssm_b_re, ssm_b_im, ssm_c_re, ssm_c_im, SSM_CHUNK)
    yg = _s5_conv(u, t_op, p_op, q_op, a_l, ssm_d, SSM_CHUNK)
    y_ssm = _matmul(yg, bf(w_glu), [(b_glu.reshape(1, pw), "row", 0), (yg, "tile", 0)], _ep_glu, BF16, 1024, 512, "s5_glu")

    mem_h = _rmsnorm(mem2, norm_mem, BF16)
    k_m = _matmul(mem_h, bf(w_mem_kv[:, :mw]), [(jnp.tile(k_norm_mem, mem_heads).reshape(1, mw), "row", 0)],
                  functools.partial(_ep_headnorm, dh=dm, scale=1.0), BF16, 256, 512, "mem_k")
    v_m = _matmul(mem_h, bf(w_mem_kv[:, mw:]), [], _ep_plain, BF16, 256, 512, "mem_v")
    y_mem = _mem_attention(qm, k_m, v_m, mem_heads, dm)

    merged = _merge(y_attn, y_ssm, y_mem, bf(w_up_attn), bf(w_up_ssm), bf(w_up_mem), gates)
    return _matmul(merged, bf(w_out), [(x2, "tile", 0)], _ep_residual, F32, 1024, 512, "out_proj")


def _moe(x1, norm_ffn, w_router, b_router, exp_w1, exp_b1, exp_w2, exp_b2, bm_e=512):
    t, d = x1.shape
    n_exp = w_router.shape[1]
    xp, idx, gate, rank, counts = _router(x1, norm_ffn, w_router, b_router)

    counts = counts.reshape(n_exp).astype(I32)
    padded = (counts + bm_e - 1) // bm_e * bm_e
    pend = jnp.cumsum(padded)
    pstart = pend - padded
    dest_flat = (pstart[idx] + rank).reshape(t * TOP_K)
    n_tiles = -(-(t * TOP_K) // bm_e) + n_exp
    n_used = (pend[-1] // bm_e).astype(I32)
    tiles = jnp.arange(n_tiles, dtype=I32)
    te = jnp.minimum(jnp.searchsorted(pend, tiles * bm_e, side="right"), n_exp - 1).astype(I32)
    te = jnp.where(tiles < n_used, te, te[n_used - 1])
    has_pad = jnp.zeros((n_tiles,), I32).at[pend // bm_e - 1].max((padded > counts).astype(I32))
    tile_fill = jnp.where(tiles < n_used, has_pad, 1)

    xs = _dispatch(xp, dest_flat, tile_fill, bm_e)
    ys = _moe_ffn(xs, te, n_used.reshape(1), exp_w1.astype(BF16), exp_b1, exp_w2.astype(BF16), exp_b2, bm_e)
    return _combine(x1, gate, dest_flat, ys)


def kernel(x, mem, norm_mix, norm_ffn, norm_mem, w_in, b_gate, b_forget, q_norm_attn, k_norm_attn, q_norm_mem, k_norm_mem, w_mem_kv, lam_re, lam_im, log_step, ssm_b_re, ssm_b_im, ssm_c_re, ssm_c_im, ssm_d, w_glu, b_glu, w_up_attn, w_up_ssm, w_up_mem, w_out, w_router, b_router, exp_w1, exp_b1, exp_w2, exp_b2):
    b, s, d = x.shape
    assert b == 1, "one sequence per call"
    x2 = x.reshape(s, d)
    for l in range(norm_mix.shape[0]):
        x2 = _mixer(x2, mem.reshape(mem.shape[1], d), norm_mix[l], norm_mem[l], w_in[l], b_gate[l], b_forget[l],
                    q_norm_attn[l], k_norm_attn[l], q_norm_mem[l], k_norm_mem[l], w_mem_kv[l], lam_re[l], lam_im[l],
                    log_step[l], ssm_b_re[l], ssm_b_im[l], ssm_c_re[l], ssm_c_im[l], ssm_d[l], w_glu[l], b_glu[l],
                    w_up_attn[l], w_up_ssm[l], w_up_mem[l], w_out[l])
        x2 = _moe(x2, norm_ffn[l], w_router[l], b_router[l], exp_w1[l], exp_b1[l], exp_w2[l], exp_b2[l])
    return x2.reshape(b, s, d)
```

```python
import functools
import math

import jax
import jax.numpy as jnp
from jax import lax
from jax.experimental import pallas as pl
from jax.experimental.pallas import tpu as pltpu

F32 = jnp.float32
BF16 = jnp.bfloat16
I32 = jnp.int32
U32 = jnp.uint32

RMS_EPS = 1e-6
TOP_K = 4
SWIGLU_ALPHA = 1.702
SWIGLU_LIMIT = 7.0

LANES = 128
VMEM_LIMIT = 56 * 1024 * 1024
SSM_CHUNK = 8
MOE_UP_CHUNK = 384
MOE_DOWN_CHUNK = 2048
HIGHEST = lax.Precision.HIGHEST
LOG2E = math.log2(math.e)
EXP2_UNDERFLOW = 160.0


def _pick_block(n, target, align=LANES):
    if n <= target:
        return n
    b = target // align * align
    while n % b:
        b -= align
    return b


def _pack_bf16_pair(lo, hi):
    a = pltpu.bitcast(lo.astype(BF16).astype(F32), U32) >> 16
    b = pltpu.bitcast(hi.astype(BF16).astype(F32), U32) & jnp.uint32(0xFFFF0000)
    return a | b


def _unpack_bf16_pair(w):
    return pltpu.bitcast(w << 16, F32), pltpu.bitcast(w & jnp.uint32(0xFFFF0000), F32)


def _params(*sem):
    return pltpu.CompilerParams(dimension_semantics=sem, vmem_limit_bytes=VMEM_LIMIT)


def _rmsnorm_kernel(x_ref, g_ref, o_ref):
    x = x_ref[...]
    ms = jnp.mean(x * x, axis=-1, keepdims=True)
    o_ref[...] = (x * lax.rsqrt(ms + RMS_EPS) * g_ref[...]).astype(o_ref.dtype)


def _rmsnorm(x, g, out_dtype, bm=256):
    m, d = x.shape
    bm = min(bm, m)
    return pl.pallas_call(
        _rmsnorm_kernel,
        grid=(m // bm,),
        in_specs=[pl.BlockSpec((bm, d), lambda i: (i, 0)), pl.BlockSpec((1, d), lambda i: (0, 0))],
        out_specs=pl.BlockSpec((bm, d), lambda i: (i, 0)),
        out_shape=jax.ShapeDtypeStruct((m, d), out_dtype),
        compiler_params=_params("parallel"),
        name="rmsnorm",
    )(x, g.reshape(1, d))


def _mm_kernel(*refs, n_extra, epilogue):
    x_ref, w_ref = refs[0], refs[1]
    extras = refs[2:2 + n_extra]
    o_ref = refs[2 + n_extra]
    acc = jnp.dot(x_ref[...].astype(BF16), w_ref[...], preferred_element_type=F32)
    o_ref[...] = epilogue(acc, *[e[...] for e in extras]).astype(o_ref.dtype)


def _matmul(x, w, extras, epilogue, out_dtype, bm, bn, name):
    m, k = x.shape
    n = w.shape[1]
    bm, bn = _pick_block(m, bm, 8), _pick_block(n, bn)
    in_specs = [pl.BlockSpec((bm, k), lambda i, j: (i, 0)), pl.BlockSpec((k, bn), lambda i, j: (0, j))]
    args = [x, w]
    for arr, kind, off in extras:
        if kind == "row":
            in_specs.append(pl.BlockSpec((1, bn), lambda i, j, off=off: (0, j + off)))
        else:
            in_specs.append(pl.BlockSpec((bm, bn), lambda i, j, off=off: (i, j + off)))
        args.append(arr)
    return pl.pallas_call(
        functools.partial(_mm_kernel, n_extra=len(extras), epilogue=epilogue),
        grid=(m // bm, n // bn),
        in_specs=in_specs,
        out_specs=pl.BlockSpec((bm, bn), lambda i, j: (i, j)),
        out_shape=jax.ShapeDtypeStruct((m, n), out_dtype),
        compiler_params=_params("parallel", "parallel"),
        name=name,
    )(*args)


def _ep_plain(acc):
    return acc


def _ep_headnorm(acc, gain, *, dh, scale):
    outs = []
    for h in range(acc.shape[1] // dh):
        blk = acc[:, h * dh:(h + 1) * dh]
        ms = jnp.mean(blk * blk, axis=-1, keepdims=True)
        outs.append(blk * lax.rsqrt(ms + RMS_EPS) * gain[:, h * dh:(h + 1) * dh] * scale)
    return outs[0] if len(outs) == 1 else jnp.concatenate(outs, axis=1)


def _ep_sigmoid_bias(acc, b):
    return jax.nn.sigmoid(acc + b)


def _ep_glu(acc, b, y):
    return y * jax.nn.sigmoid(acc + b)


def _ep_residual(acc, r):
    return acc + r


def _split3(x):
    hi = x.astype(BF16)
    r1 = x - hi.astype(F32)
    mid = r1.astype(BF16)
    lo = (r1 - mid.astype(F32)).astype(BF16)
    return hi, mid, lo


def _forget_cumsum_kernel(f_ref, b_ref, c_ref, hi_ref, mid_ref, lo_ref, carry_ref, *, bs):
    @pl.when(pl.program_id(0) == 0)
    def _():
        carry_ref[...] = jnp.zeros_like(carry_ref)

    z = f_ref[...] + b_ref[...]
    logf = jnp.minimum(z, 0.0) - jnp.log1p(jnp.exp(-jnp.abs(z)))
    row = lax.broadcasted_iota(I32, (bs, bs), 0)
    col = lax.broadcasted_iota(I32, (bs, bs), 1)
    tri = jnp.where(col <= row, 1.0, 0.0).astype(BF16)
    hi, mid, lo = _split3(logf)
    cs = (jnp.dot(tri, hi, preferred_element_type=F32) + jnp.dot(tri, mid, preferred_element_type=F32)
          + jnp.dot(tri, lo, preferred_element_type=F32)) + carry_ref[...]
    carry_ref[...] = cs[bs - 1:bs, :]
    c2 = cs * LOG2E
    c_ref[...] = c2
    hi_ref[...], mid_ref[...], lo_ref[...] = _split3(c2)


def _forget_cumsum(f_logit, b_forget_pad, bs=256):
    s, w = f_logit.shape
    bs = min(bs, s)
    blk = lambda: pl.BlockSpec((bs, w), lambda i: (i, 0))
    return pl.pallas_call(
        functools.partial(_forget_cumsum_kernel, bs=bs),
        grid=(s // bs,),
        in_specs=[blk(), pl.BlockSpec((1, w), lambda i: (0, 0))],
        out_specs=[blk(), blk(), blk(), blk()],
        out_shape=[jax.ShapeDtypeStruct((s, w), F32)] + [jax.ShapeDtypeStruct((s, w), BF16)] * 3,
        scratch_shapes=[pltpu.VMEM((1, w), F32)],
        compiler_params=_params("arbitrary"),
        name="forget_cumsum",
    )(f_logit, b_forget_pad)


def _fox_kernel(jstart_ref, qt_ref, k_ref, vt_ref, o_ref, m_ref, l_ref, acc_ref, *, nb):
    h, i = pl.program_id(0), pl.program_id(1)
    qt = qt_ref[...]

    def scores(j):
        return jnp.dot(k_ref[j], qt, preferred_element_type=F32)

    st = scores(i)
    kpos = lax.broadcasted_iota(I32, st.shape, 0)
    qpos = lax.broadcasted_iota(I32, st.shape, 1)
    st = jnp.where(kpos <= qpos, st, -jnp.inf)
    m0 = jnp.max(st, axis=0, keepdims=True)
    p = jnp.exp2(st - m0)
    m_ref[...] = m0
    l_ref[...] = jnp.sum(p, axis=0, keepdims=True)
    acc_ref[...] = jnp.dot(vt_ref[i], p.astype(BF16), preferred_element_type=F32)

    def body(t, carry):
        j = i - 1 - t
        st = scores(j)
        m_old = m_ref[...]
        m_new = jnp.maximum(m_old, jnp.max(st, axis=0, keepdims=True))
        p = jnp.exp2(st - m_new)
        corr = jnp.exp2(m_old - m_new)
        l_ref[...] = l_ref[...] * corr + jnp.sum(p, axis=0, keepdims=True)
        acc_ref[...] = acc_ref[...] * corr + jnp.dot(vt_ref[j], p.astype(BF16), preferred_element_type=F32)
        m_ref[...] = m_new
        return carry

    lax.fori_loop(0, i - jstart_ref[h * nb + i], body, 0)
    o_ref[...] = (acc_ref[...] * (1.0 / l_ref[...])).astype(o_ref.dtype)


def _fox_attention(q, k, v, c2, c_pieces, logit_bound, n_heads, dh, blk=512):
    s = q.shape[0]
    blk = min(blk, s)
    nb = s // blk
    ka = 2 * dh
    pieces = jnp.stack([p[:, :n_heads] for p in c_pieces], axis=-1)
    ones = jnp.ones((s, n_heads, 3), BF16)
    zpad = jnp.zeros((s, n_heads, ka - dh - 6), BF16)
    q_aug = jnp.concatenate([q.reshape(s, n_heads, dh), pieces, ones, zpad], axis=-1)
    k_aug = jnp.concatenate([k.reshape(s, n_heads, dh), ones, -pieces, zpad], axis=-1)
    qt = jnp.transpose(q_aug, (1, 2, 0))
    k4 = jnp.transpose(k_aug.reshape(nb, blk, n_heads, ka), (2, 0, 1, 3))
    vt = jnp.transpose(v.reshape(nb, blk, n_heads, dh), (2, 0, 3, 1))

    ct = jnp.transpose(c2[:, :n_heads]).reshape(n_heads, nb, blk)
    max_bias = ct[:, :, None, 0] - ct[:, None, :, blk - 1]
    ii = jnp.arange(nb)
    needed = jnp.logical_and(max_bias >= -(2.0 * logit_bound + EXP2_UNDERFLOW), ii[None, None, :] <= ii[None, :, None])
    jstart = (ii[None, :] + 1 - jnp.sum(needed, axis=-1)).astype(I32).reshape(n_heads * nb)

    return pl.pallas_call(
        functools.partial(_fox_kernel, nb=nb),
        grid_spec=pltpu.PrefetchScalarGridSpec(
            num_scalar_prefetch=1,
            grid=(n_heads, nb),
            in_specs=[
                pl.BlockSpec((None, ka, blk), lambda h, i, js: (h, 0, i)),
                pl.BlockSpec((None, nb, blk, ka), lambda h, i, js: (h, 0, 0, 0)),
                pl.BlockSpec((None, nb, dh, blk), lambda h, i, js: (h, 0, 0, 0)),
            ],
            out_specs=pl.BlockSpec((dh, blk), lambda h, i, js: (h, i)),
            scratch_shapes=[pltpu.VMEM((1, blk), F32), pltpu.VMEM((1, blk), F32), pltpu.VMEM((dh, blk), F32)],
        ),
        out_shape=jax.ShapeDtypeStruct((n_heads * dh, s), BF16),
        compiler_params=_params("parallel", "arbitrary"),
        name="fox_attention",
    )(jstart, qt, k4, vt)


def _s5_params_kernel(lre_ref, lim_ref, ls_ref, bre_ref, bim_ref, ctre_ref, ctim_ref, cre_ref, cim_ref,
                      zre_ref, zim_ref, wre_ref, wim_ref, m_ref, alre_ref, alim_ref, *, chunk, n_ch):
    lre, lim = lre_ref[...], lim_ref[...]
    step = jnp.exp(ls_ref[...])
    ar, ai = lre * step, lim * step
    lc = chunk * n_ch
    tau = (lax.broadcasted_iota(I32, (1, 1, lc), 2) // n_ch).astype(F32)

    def cexp(t):
        mag = jnp.exp(ar * t)
        return mag * jnp.cos(ai * t), mag * jnp.sin(ai * t)

    e1r, e1i = cexp(1.0)
    nr, ni = e1r - 1.0, e1i
    den = lre * lre + lim * lim
    cr, ci = (nr * lre + ni * lim) / den, (ni * lre - nr * lim) / den
    bre, bim = bre_ref[...], bim_ref[...]
    bbr, bbi = cr * bre - ci * bim, cr * bim + ci * bre
    er, ei = cexp(tau)
    zr, zi = er * bbr - ei * bbi, er * bbi + ei * bbr
    zre_ref[...] = zr
    zim_ref[...] = zi
    e1r_t, e1i_t = cexp(tau + 1.0)
    ctre, ctim = ctre_ref[...], ctim_ref[...]
    wre_ref[...] = ctre * e1r_t - ctim * e1i_t
    wim_ref[...] = ctre * e1i_t + ctim * e1r_t
    dn = (((2,), (1,)), ((0,), (0,)))
    m_ref[...] = (lax.dot_general(cre_ref[...], zr, dn, precision=HIGHEST, preferred_element_type=F32)
                  - lax.dot_general(cim_ref[...], zi, dn, precision=HIGHEST, preferred_element_type=F32))
    alr, ali = cexp(float(chunk))
    alre_ref[...] = alr
    alim_ref[...] = ali


def _s5_operators(lam_re, lam_im, log_step, b_re, b_im, c_re, c_im, chunk):
    g, n = lam_re.shape
    n_ch = b_re.shape[-1]
    gs = LANES // n_ch
    assert g % gs == 0
    nsl, lc = g // gs, chunk * n_ch
    tile = lambda a: jnp.tile(a, (1, 1, chunk))
    ct_re, ct_im = jnp.transpose(c_re, (0, 2, 1)), jnp.transpose(c_im, (0, 2, 1))
    spec3 = lambda a, b: pl.BlockSpec((gs, a, b), lambda i: (i, 0, 0))
    zre, zim, wre, wim, mk, alre, alim = pl.pallas_call(
        functools.partial(_s5_params_kernel, chunk=chunk, n_ch=n_ch),
        grid=(nsl,),
        in_specs=[spec3(n, 1), spec3(n, 1), spec3(1, 1), spec3(n, lc), spec3(n, lc), spec3(n, lc), spec3(n, lc),
                  spec3(n_ch, n), spec3(n_ch, n)],
        out_specs=[spec3(n, lc), spec3(n, lc), spec3(n, lc), spec3(n, lc), spec3(n_ch, lc), spec3(n, 1), spec3(n, 1)],
        out_shape=[jax.ShapeDtypeStruct((g, n, lc), F32)] * 4 + [jax.ShapeDtypeStruct((g, n_ch, lc), F32)]
        + [jax.ShapeDtypeStruct((g, n, 1), F32)] * 2,
        compiler_params=_params("parallel"),
        name="s5_params",
    )(lam_re.reshape(g, n, 1), lam_im.reshape(g, n, 1), log_step.reshape(g, 1, 1), tile(b_re), tile(b_im),
      tile(ct_re), tile(ct_im), c_re, c_im)

    eye = jnp.eye(gs, dtype=F32)
    m5 = mk.reshape(nsl, gs, n_ch, chunk, n_ch)
    kk = jnp.transpose(m5, (0, 3, 1, 4, 2))
    kk = (kk[:, :, :, :, None, :] * eye[None, None, :, None, :, None]).reshape(nsl, chunk, LANES, LANES)
    d = jnp.arange(chunk)[None, :] - jnp.arange(chunk)[:, None]
    tm = jnp.where((d >= 0)[None, :, :, None, None], kk[:, jnp.clip(d, 0, chunk - 1)], 0.0)
    t_op = jnp.transpose(tm, (0, 1, 3, 2, 4)).reshape(nsl, chunk * LANES, chunk * LANES).astype(BF16)

    def inject(z):
        z5 = z.reshape(nsl, gs, n, chunk, n_ch)[:, :, :, ::-1, :]
        pd = jnp.transpose(z5, (0, 3, 1, 4, 2))
        return (pd[:, :, :, :, None, :] * eye[None, None, :, None, :, None]).reshape(nsl, chunk * LANES, gs * n)

    def readout(w):
        w5 = w.reshape(nsl, gs, n, chunk, n_ch)
        return (w5[:, :, :, :, None, :] * eye[None, :, None, None, :, None]).reshape(nsl, gs * n, chunk * LANES)

    p_op = jnp.concatenate([inject(zre), inject(zim)], axis=-1).astype(BF16)
    q_op = jnp.concatenate([readout(wre), -readout(wim)], axis=1).astype(BF16)
    a_l = jnp.concatenate([alre.reshape(nsl, 1, gs * n), alim.reshape(nsl, 1, gs * n)], axis=-1)
    return t_op, p_op, q_op, a_l


def _gelu_tanh(x):
    return 0.5 * x * (1.0 + jnp.tanh(math.sqrt(2.0 / math.pi) * (x + 0.044715 * (x * x * x))))


def _s5_conv_kernel(u_ref, t_ref, p_ref, q_ref, a_ref, d_ref, o_ref, hst_ref, s_ref, hp_ref, *, chunk, bk):
    hn = a_ref.shape[1] // 2

    @pl.when(pl.program_id(1) == 0)
    def _():
        hst_ref[...] = jnp.zeros_like(hst_ref)

    x = jnp.concatenate([u_ref[pl.ds(s, bk, stride=chunk), :] for s in range(chunk)], axis=1).astype(BF16)
    s_ref[...] = jnp.dot(x, p_ref[...], preferred_element_type=F32)
    ar, ai = a_ref[:, :hn], a_ref[:, hn:]

    def body(k, h):
        hr, hi = h
        hp_ref[pl.ds(k, 1), :hn] = hr
        hp_ref[pl.ds(k, 1), hn:] = hi
        row = s_ref[pl.ds(k, 1), :]
        return ar * hr - ai * hi + row[:, :hn], ar * hi + ai * hr + row[:, hn:]

    hr, hi = lax.fori_loop(0, bk, body, (hst_ref[:, :hn], hst_ref[:, hn:]))
    hst_ref[:, :hn] = hr
    hst_ref[:, hn:] = hi

    y = (jnp.dot(x, t_ref[...], preferred_element_type=F32)
         + jnp.dot(hp_ref[...].astype(BF16), q_ref[...], preferred_element_type=F32))
    d = d_ref[...]
    for t in range(chunk):
        yt = y[:, t * LANES:(t + 1) * LANES] + d * u_ref[pl.ds(t, bk, stride=chunk), :]
        o_ref[pl.ds(t, bk, stride=chunk), :] = _gelu_tanh(yt)


def _s5_conv(u, t_op, p_op, q_op, a_l, d_skip, chunk, bk=512):
    s, p = u.shape
    nsl = p // LANES
    bk = min(bk, s // chunk)
    bm = bk * chunk
    hn2 = a_l.shape[-1]
    return pl.pallas_call(
        functools.partial(_s5_conv_kernel, chunk=chunk, bk=bk),
        grid=(nsl, s // bm),
        in_specs=[
            pl.BlockSpec((bm, LANES), lambda sl, i: (i, sl)),
            pl.BlockSpec((None, chunk * LANES, chunk * LANES), lambda sl, i: (sl, 0, 0)),
            pl.BlockSpec((None, chunk * LANES, hn2), lambda sl, i: (sl, 0, 0)),
            pl.BlockSpec((None, hn2, chunk * LANES), lambda sl, i: (sl, 0, 0)),
            pl.BlockSpec((None, 1, hn2), lambda sl, i: (sl, 0, 0)),
            pl.BlockSpec((1, LANES), lambda sl, i: (0, sl)),
        ],
        out_specs=pl.BlockSpec((bm, LANES), lambda sl, i: (i, sl)),
        out_shape=jax.ShapeDtypeStruct((s, p), F32),
        scratch_shapes=[pltpu.VMEM((1, hn2), F32), pltpu.VMEM((bk, hn2), F32), pltpu.VMEM((bk, hn2), F32)],
        compiler_params=_params("parallel", "arbitrary"),
        name="s5_conv",
    )(u, t_op, p_op, q_op, a_l, d_skip.reshape(1, p))


def _mem_attn_kernel(q_ref, k_ref, v_ref, o_ref, *, n_heads, dm):
    for h in range(n_heads):
        sl = slice(h * dm, (h + 1) * dm)
        s = lax.dot_general(q_ref[:, sl], k_ref[:, sl], (((1,), (1,)), ((), ())), preferred_element_type=F32)
        p = jnp.exp(s - jnp.max(s, axis=-1, keepdims=True))
        o = jnp.dot(p.astype(BF16), v_ref[:, sl], preferred_element_type=F32) / jnp.sum(p, axis=-1, keepdims=True)
        o_ref[:, sl] = o.astype(o_ref.dtype)


def _mem_attention(q, k, v, n_heads, dm, bm=512):
    s, w = q.shape
    m = k.shape[0]
    bm = min(bm, s)
    return pl.pallas_call(
        functools.partial(_mem_attn_kernel, n_heads=n_heads, dm=dm),
        grid=(s // bm,),
        in_specs=[pl.BlockSpec((bm, w), lambda i: (i, 0)), pl.BlockSpec((m, w), lambda i: (0, 0)),
                  pl.BlockSpec((m, w), lambda i: (0, 0))],
        out_specs=pl.BlockSpec((bm, w), lambda i: (i, 0)),
        out_shape=jax.ShapeDtypeStruct((s, w), BF16),
        compiler_params=_params("parallel"),
        name="mem_attention",
    )(q, k, v)


def _merge_kernel(ya_ref, ys_ref, ym_ref, wa_ref, ws_ref, wm_ref, ga_ref, gs_ref, gm_ref, o_ref):
    a = jnp.dot(ya_ref[...], wa_ref[...], preferred_element_type=F32)
    s = jnp.dot(ys_ref[...], ws_ref[...], preferred_element_type=F32)
    m = jnp.dot(ym_ref[...], wm_ref[...], preferred_element_type=F32)
    o_ref[...] = (ga_ref[...].astype(F32) * a + gs_ref[...].astype(F32) * s
                  + gm_ref[...].astype(F32) * m).astype(o_ref.dtype)


def _merge(ya, ys, ym, wa, ws, wm, gates, bm=1024, bn=512):
    s, d = ya.shape[0], wa.shape[1]
    bm, bn = min(bm, s), min(bn, d)
    nb = d // bn
    row = lambda a: pl.BlockSpec((bm, a.shape[1]), lambda i, j: (i, 0))
    col = lambda a: pl.BlockSpec((a.shape[0], bn), lambda i, j: (0, j))
    gate = lambda b: pl.BlockSpec((bm, bn), lambda i, j, b=b: (i, j + b * nb))
    return pl.pallas_call(
        _merge_kernel,
        grid=(s // bm, nb),
        in_specs=[row(ya), row(ys), row(ym), col(wa), col(ws), col(wm), gate(0), gate(1), gate(2)],
        out_specs=pl.BlockSpec((bm, bn), lambda i, j: (i, j)),
        out_shape=jax.ShapeDtypeStruct((s, d), BF16),
        compiler_params=_params("parallel", "parallel"),
        name="merge",
    )(ya, ys, ym, wa, ws, wm, gates, gates, gates)


def _router_kernel(x_ref, g_ref, wr_ref, br_ref, xp_ref, idx_ref, gate_ref, rank_ref, cnt_ref, carry_ref, *, bm):
    @pl.when(pl.program_id(0) == 0)
    def _():
        carry_ref[...] = jnp.zeros_like(carry_ref)

    x = x_ref[...]
    ms = jnp.mean(x * x, axis=-1, keepdims=True)
    xn = x * lax.rsqrt(ms + RMS_EPS) * g_ref[...]
    d2 = xn.shape[1] // 2
    xp_ref[...] = _pack_bf16_pair(xn[:, :d2], xn[:, d2:])

    n_exp = wr_ref.shape[1]
    logits = jnp.dot(xn, wr_ref[...], precision=HIGHEST, preferred_element_type=F32) + br_ref[...]
    lane = lax.broadcasted_iota(I32, logits.shape, 1)
    kcol = lax.broadcasted_iota(I32, (bm, TOP_K), 1)
    work = logits
    member = jnp.zeros_like(logits)
    idxs, vals = [], []
    for _ in range(TOP_K):
        mx = jnp.max(work, axis=-1, keepdims=True)
        ik = jnp.min(jnp.where(work == mx, lane, n_exp), axis=-1, keepdims=True)
        sel = lane == ik
        member = jnp.where(sel, 1.0, member)
        work = jnp.where(sel, -jnp.inf, work)
        idxs.append(ik)
        vals.append(mx)
    es = [jnp.exp(v - vals[0]) for v in vals]
    den = es[0] + es[1] + es[2] + es[3]

    r = lax.broadcasted_iota(I32, (bm, bm), 0)
    c = lax.broadcasted_iota(I32, (bm, bm), 1)
    tri = jnp.where(c < r, 1.0, 0.0).astype(BF16)
    prefix = jnp.dot(tri, member.astype(BF16), preferred_element_type=F32) + carry_ref[...]
    total = carry_ref[...] + jnp.sum(member, axis=0, keepdims=True)
    carry_ref[...] = total
    cnt_ref[...] = total

    idx_o = jnp.zeros((bm, TOP_K), I32)
    gate_o = jnp.zeros((bm, TOP_K), F32)
    rank_o = jnp.zeros((bm, TOP_K), F32)
    for k in range(TOP_K):
        rk = jnp.sum(jnp.where(lane == idxs[k], prefix, 0.0), axis=-1, keepdims=True)
        idx_o = jnp.where(kcol == k, idxs[k], idx_o)
        gate_o = jnp.where(kcol == k, es[k] / den, gate_o)
        rank_o = jnp.where(kcol == k, rk, rank_o)
    idx_ref[...] = idx_o
    gate_ref[...] = gate_o
    rank_ref[...] = rank_o.astype(I32)


def _router(x1, g, w_router, b_router, bm=256):
    t, d = x1.shape
    n_exp = w_router.shape[1]
    bm = min(bm, t)
    small = lambda: pl.BlockSpec((bm, TOP_K), lambda i: (i, 0))
    return pl.pallas_call(
        functools.partial(_router_kernel, bm=bm),
        grid=(t // bm,),
        in_specs=[pl.BlockSpec((bm, d), lambda i: (i, 0)), pl.BlockSpec((1, d), lambda i: (0, 0)),
                  pl.BlockSpec((d, n_exp), lambda i: (0, 0)), pl.BlockSpec((1, n_exp), lambda i: (0, 0))],
        out_specs=[pl.BlockSpec((bm, d // 2), lambda i: (i, 0)), small(), small(), small(),
                   pl.BlockSpec((1, n_exp), lambda i: (0, 0))],
        out_shape=[jax.ShapeDtypeStruct((t, d // 2), U32), jax.ShapeDtypeStruct((t, TOP_K), I32),
                   jax.ShapeDtypeStruct((t, TOP_K), F32), jax.ShapeDtypeStruct((t, TOP_K), I32),
                   jax.ShapeDtypeStruct((1, n_exp), F32)],
        scratch_shapes=[pltpu.VMEM((1, n_exp), F32)],
        compiler_params=_params("arbitrary"),
        name="router",
    )(x1, g.reshape(1, d), w_router, b_router.reshape(1, n_exp))


def _dispatch_kernel(fill_ref, dest_ref, xp_ref, xs_ref, zero_ref, sem_z, sem_r, *, bm, bm_e, n_tiles):
    @pl.when(pl.program_id(0) == 0)
    def _():
        zero_ref[...] = jnp.zeros_like(zero_ref)

        def fill(t, wait):
            @pl.when(fill_ref[t] > 0)
            def _():
                cp = pltpu.make_async_copy(zero_ref, xs_ref.at[pl.ds(t * bm_e, bm_e), :], sem_z)
                cp.wait() if wait else cp.start()

        lax.fori_loop(0, n_tiles, lambda t, c: (fill(t, False), c)[1], 0)
        lax.fori_loop(0, n_tiles, lambda t, c: (fill(t, True), c)[1], 0)

    def row_copy(r, k):
        return pltpu.make_async_copy(xp_ref.at[pl.ds(r, 1), :], xs_ref.at[pl.ds(dest_ref[r * TOP_K + k], 1), :], sem_r)

    def issue(r, c):
        for k in range(TOP_K):
            row_copy(r, k).start()
        return c

    def drain(r, c):
        for k in range(TOP_K):
            row_copy(r, k).wait()
        return c

    lax.fori_loop(0, bm, issue, 0)
    lax.fori_loop(0, bm, drain, 0)


def _dispatch(xp, dest_flat, tile_fill, bm_e, bm=256):
    t, d2 = xp.shape
    bm = min(bm, t)
    n_tiles = tile_fill.shape[0]
    n_rows = n_tiles * bm_e
    return pl.pallas_call(
        functools.partial(_dispatch_kernel, bm=bm, bm_e=bm_e, n_tiles=n_tiles),
        grid_spec=pltpu.PrefetchScalarGridSpec(
            num_scalar_prefetch=1,
            grid=(t // bm,),
            in_specs=[pl.BlockSpec((bm * TOP_K,), lambda i, lt: (i,), memory_space=pltpu.SMEM),
                      pl.BlockSpec((bm, d2), lambda i, lt: (i, 0))],
            out_specs=pl.BlockSpec(memory_space=pl.ANY),
            scratch_shapes=[pltpu.VMEM((bm_e, d2), U32), pltpu.SemaphoreType.DMA(()), pltpu.SemaphoreType.DMA(())],
        ),
        out_shape=jax.ShapeDtypeStruct((n_rows, d2), U32),
        compiler_params=_params("arbitrary"),
        name="moe_dispatch",
    )(tile_fill, dest_flat, xp)


FIRST_VISIT, TILE_USED = 1, 2


def _work_list(tile_start, tile_count, n_used, n_tiles, n_chunks):
    n_exp = tile_start.shape[0]
    w = jnp.arange(n_tiles * n_chunks, dtype=I32)
    e = jnp.minimum(jnp.sum((n_chunks * (tile_start + tile_count))[None, :] <= w[:, None], axis=1), n_exp - 1).astype(I32)
    cnt = jnp.maximum(tile_count[e], 1)
    r = w - n_chunks * tile_start[e]
    used = w < n_chunks * n_used
    n_unused = jnp.maximum(n_tiles - n_used, 1)
    r2 = w - n_chunks * n_used
    tile = jnp.where(used, tile_start[e] + r % cnt, n_used + r2 % n_unused)
    chunk = jnp.where(used, r // cnt, r2 // n_unused)
    last = n_chunks * n_used - 1
    in_tile = jnp.where(used, tile, tile[last])
    w_exp = jnp.where(used, e, e[last])
    w_chunk = jnp.where(used, chunk, chunk[last])
    flags = jnp.where(used, TILE_USED + jnp.where(r % cnt == 0, FIRST_VISIT, 0), 0)
    return [a.astype(I32) for a in (tile, chunk, in_tile, w_exp, w_chunk, flags)]


def _moe_up_kernel(tile_ref, chunk_ref, xin_ref, we_ref, wc_ref, flag_ref, xs_ref, wg_ref, wl_ref, bg_ref, bl_ref,
                   o_ref, wgb_ref, wlb_ref):
    flag = flag_ref[pl.program_id(0)]

    @pl.when((flag & FIRST_VISIT) != 0)
    def _():
        wgb_ref[...] = wg_ref[...].astype(BF16)
        wlb_ref[...] = wl_ref[...].astype(BF16)

    @pl.when((flag & TILE_USED) != 0)
    def _():
        lo, hi = _unpack_bf16_pair(xs_ref[...])
        xlo, xhi = lo.astype(BF16), hi.astype(BF16)
        d2 = xlo.shape[1]
        glu = (jnp.dot(xlo, wgb_ref[:d2, :], preferred_element_type=F32)
               + jnp.dot(xhi, wgb_ref[d2:, :], preferred_element_type=F32) + bg_ref[...])
        lin = (jnp.dot(xlo, wlb_ref[:d2, :], preferred_element_type=F32)
               + jnp.dot(xhi, wlb_ref[d2:, :], preferred_element_type=F32) + bl_ref[...])
        glu = jnp.minimum(glu, SWIGLU_LIMIT)
        lin = jnp.clip(lin, -SWIGLU_LIMIT, SWIGLU_LIMIT)
        o_ref[...] = (glu * jax.nn.sigmoid(SWIGLU_ALPHA * glu) * (lin + 1.0)).astype(o_ref.dtype)

    @pl.when((flag & TILE_USED) == 0)
    def _():
        o_ref[...] = jnp.zeros_like(o_ref)


def _moe_up(xs, work, w1, b1, bm_e, tf):
    n_rows, d2 = xs.shape
    n_exp, d, f2 = w1.shape
    ff = f2 // 2
    nf = ff // tf
    wspec = lambda off: pl.BlockSpec((None, d, tf), lambda w, t, c, xi, we, wc, fl: (we[w], 0, wc[w] + off))
    bspec = lambda off: pl.BlockSpec((None, 1, tf), lambda w, t, c, xi, we, wc, fl: (we[w], 0, wc[w] + off))
    return pl.pallas_call(
        _moe_up_kernel,
        grid_spec=pltpu.PrefetchScalarGridSpec(
            num_scalar_prefetch=6,
            grid=(work[0].shape[0],),
            in_specs=[pl.BlockSpec((bm_e, d2), lambda w, t, c, xi, we, wc, fl: (xi[w], 0)),
                      wspec(0), wspec(nf), bspec(0), bspec(nf)],
            out_specs=pl.BlockSpec((bm_e, tf), lambda w, t, c, xi, we, wc, fl: (t[w], c[w])),
            scratch_shapes=[pltpu.VMEM((d, tf), BF16), pltpu.VMEM((d, tf), BF16)],
        ),
        out_shape=jax.ShapeDtypeStruct((n_rows, ff), BF16),
        compiler_params=_params("arbitrary"),
        name="moe_up",
    )(*work, xs, w1, w1, b1.reshape(n_exp, 1, f2), b1.reshape(n_exp, 1, f2))


def _moe_down_kernel(tile_ref, chunk_ref, hin_ref, we_ref, wc_ref, flag_ref, h_ref, w2_ref, b2_ref, o_ref, w2b_ref):
    flag = flag_ref[pl.program_id(0)]

    @pl.when((flag & FIRST_VISIT) != 0)
    def _():
        w2b_ref[...] = w2_ref[...].astype(BF16)

    @pl.when((flag & TILE_USED) != 0)
    def _():
        y = jnp.dot(h_ref[...], w2b_ref[...], preferred_element_type=F32) + b2_ref[...]
        half = y.shape[1] // 2
        o_ref[...] = _pack_bf16_pair(y[:, :half], y[:, half:])

    @pl.when((flag & TILE_USED) == 0)
    def _():
        o_ref[...] = jnp.zeros_like(o_ref)


def _moe_down(hid, work, w2, b2, bm_e, tn):
    n_rows, ff = hid.shape
    n_exp, _, d = w2.shape
    return pl.pallas_call(
        _moe_down_kernel,
        grid_spec=pltpu.PrefetchScalarGridSpec(
            num_scalar_prefetch=6,
            grid=(work[0].shape[0],),
            in_specs=[pl.BlockSpec((bm_e, ff), lambda w, t, c, hi, we, wc, fl: (hi[w], 0)),
                      pl.BlockSpec((None, ff, tn), lambda w, t, c, hi, we, wc, fl: (we[w], 0, wc[w])),
                      pl.BlockSpec((None, 1, tn), lambda w, t, c, hi, we, wc, fl: (we[w], 0, wc[w]))],
            out_specs=pl.BlockSpec((bm_e, tn // 2), lambda w, t, c, hi, we, wc, fl: (t[w], c[w])),
            scratch_shapes=[pltpu.VMEM((ff, tn), BF16)],
        ),
        out_shape=jax.ShapeDtypeStruct((n_rows, d // 2), U32),
        compiler_params=_params("arbitrary"),
        name="moe_down",
    )(*work, hid, w2, b2.reshape(n_exp, 1, d))


def _combine_kernel(dest_ref, x_ref, gate_ref, ys_ref, o_ref, buf_ref, sem, *, bm, tn):
    def row_copy(r, k):
        return pltpu.make_async_copy(ys_ref.at[pl.ds(dest_ref[r * TOP_K + k], 1), :],
                                     buf_ref.at[k, pl.ds(r, 1), :], sem)

    def issue(r, c):
        for k in range(TOP_K):
            row_copy(r, k).start()
        return c

    def drain(r, c):
        for k in range(TOP_K):
            row_copy(r, k).wait()
        return c

    lax.fori_loop(0, bm, issue, 0)
    lax.fori_loop(0, bm, drain, 0)
    g = gate_ref[...]
    half = tn // 2
    for n in range(x_ref.shape[1] // tn):
        acc_lo = x_ref[:, n * tn:n * tn + half]
        acc_hi = x_ref[:, n * tn + half:(n + 1) * tn]
        for k in range(TOP_K):
            lo, hi = _unpack_bf16_pair(buf_ref[k, :, n * half:(n + 1) * half])
            acc_lo = acc_lo + g[:, k:k + 1] * lo
            acc_hi = acc_hi + g[:, k:k + 1] * hi
        o_ref[:, n * tn:n * tn + half] = acc_lo
        o_ref[:, n * tn + half:(n + 1) * tn] = acc_hi


def _combine(x1, gate, dest_flat, ys, tn, bm=128):
    t, d = x1.shape
    bm = min(bm, t)
    return pl.pallas_call(
        functools.partial(_combine_kernel, bm=bm, tn=tn),
        grid=(t // bm,),
        in_specs=[pl.BlockSpec((bm * TOP_K,), lambda i: (i,), memory_space=pltpu.SMEM),
                  pl.BlockSpec((bm, d), lambda i: (i, 0)),
                  pl.BlockSpec((bm, TOP_K), lambda i: (i, 0)),
                  pl.BlockSpec(memory_space=pl.ANY)],
        out_specs=pl.BlockSpec((bm, d), lambda i: (i, 0)),
        out_shape=jax.ShapeDtypeStruct((t, d), F32),
        scratch_shapes=[pltpu.VMEM((TOP_K, bm, d // 2), U32), pltpu.SemaphoreType.DMA(())],
        compiler_params=_params("arbitrary"),
        name="moe_combine",
    )(dest_flat, x1, gate, ys)


def _mixer(x2, mem2, norm_mix, norm_mem, w_in, b_gate, b_forget, q_norm_attn, k_norm_attn, q_norm_mem, k_norm_mem,
           w_mem_kv, lam_re, lam_im, log_step, ssm_b_re, ssm_b_im, ssm_c_re, ssm_c_im, ssm_d, w_glu, b_glu,
           w_up_attn, w_up_ssm, w_up_mem, w_out):
    s, d = x2.shape
    n_heads, dh = b_forget.shape[0], q_norm_attn.shape[0]
    aw = n_heads * dh
    pw = ssm_d.shape[0]
    dm = q_norm_mem.shape[0]
    mw = w_up_mem.shape[0]
    mem_heads = mw // dm
    o_k, o_v, o_f, o_u, o_qm, o_g = aw, 2 * aw, 3 * aw, 3 * aw + n_heads, 3 * aw + n_heads + pw, 3 * aw + n_heads + pw + mw
    bf = lambda a: a.astype(BF16)

    h = _rmsnorm(x2, norm_mix, BF16)
    q = _matmul(h, bf(w_in[:, :o_k]), [(jnp.tile(q_norm_attn, n_heads).reshape(1, aw), "row", 0)],
                functools.partial(_ep_headnorm, dh=dh, scale=dh ** -0.5 * LOG2E), BF16, 1024, 512, "in_q")
    k = _matmul(h, bf(w_in[:, o_k:o_v]), [(jnp.tile(k_norm_attn, n_heads).reshape(1, aw), "row", 0)],
                functools.partial(_ep_headnorm, dh=dh, scale=1.0), BF16, 1024, 512, "in_k")
    v = _matmul(h, bf(w_in[:, o_v:o_f]), [], _ep_plain, BF16, 1024, 512, "in_v")
    w_f = jnp.pad(w_in[:, o_f:o_u], ((0, 0), (0, LANES - n_heads)))
    f_logit = _matmul(h, bf(w_f), [], _ep_plain, F32, 1024, LANES, "in_f")
    u = _matmul(h, bf(w_in[:, o_u:o_qm]), [], _ep_plain, F32, 1024, 512, "in_u")
    qm = _matmul(h, bf(w_in[:, o_qm:o_g]), [(jnp.tile(q_norm_mem, mem_heads).reshape(1, mw), "row", 0)],
                 functools.partial(_ep_headnorm, dh=dm, scale=dm ** -0.5), BF16, 1024, 512, "in_qm")
    gates = _matmul(h, bf(w_in[:, o_g:]), [(b_gate.reshape(1, 3 * d), "row", 0)], _ep_sigmoid_bias, BF16,
                    1024, 1024, "in_gates")

    c2, *c_pieces = _forget_cumsum(f_logit, jnp.pad(b_forget, (0, LANES - n_heads)).reshape(1, LANES))
    logit_bound = 1.02 * LOG2E * dh ** 0.5 * jnp.max(jnp.abs(q_norm_attn)) * jnp.max(jnp.abs(k_norm_attn))
    y_attn = jnp.transpose(_fox_attention(q, k, v, c2, c_pieces, logit_bound, n_heads, dh))

    t_op, p_op, q_op, a_l = _s5_operators(lam_re, lam_im, log_step, ssm_b_re, ssm_b_im, ssm_c_re, ssm_c_im, SSM_CHUNK)
    yg = _s5_conv(u, t_op, p_op, q_op, a_l, ssm_d, SSM_CHUNK)
    y_ssm = _matmul(yg, bf(w_glu), [(b_glu.reshape(1, pw), "row", 0), (yg, "tile", 0)], _ep_glu, BF16, 1024, 512, "s5_glu")

    mem_h = _rmsnorm(mem2, norm_mem, BF16)
    k_m = _matmul(mem_h, bf(w_mem_kv[:, :mw]), [(jnp.tile(k_norm_mem, mem_heads).reshape(1, mw), "row", 0)],
                  functools.partial(_ep_headnorm, dh=dm, scale=1.0), BF16, 256, 512, "mem_k")
    v_m = _matmul(mem_h, bf(w_mem_kv[:, mw:]), [], _ep_plain, BF16, 256, 512, "mem_v")
    y_mem = _mem_attention(qm, k_m, v_m, mem_heads, dm)

    merged = _merge(y_attn, y_ssm, y_mem, bf(w_up_attn), bf(w_up_ssm), bf(w_up_mem), gates)
    return _matmul(merged, bf(w_out), [(x2, "tile", 0)], _ep_residual, F32, 1024, 512, "out_proj")


def _moe(x1, norm_ffn, w_router, b_router, exp_w1, exp_b1, exp_w2, exp_b2, bm_e=512):
    t, d = x1.shape
    n_exp = w_router.shape[1]
    xp, idx, gate, rank, counts = _router(x1, norm_ffn, w_router, b_router)

    counts = counts.reshape(n_exp).astype(I32)
    padded = (counts + bm_e - 1) // bm_e * bm_e
    pend = jnp.cumsum(padded)
    pstart = pend - padded
    dest_flat = (pstart[idx] + rank).reshape(t * TOP_K)
    n_tiles = -(-(t * TOP_K) // bm_e) + n_exp
    n_used = (pend[-1] // bm_e).astype(I32)
    tiles = jnp.arange(n_tiles, dtype=I32)
    has_pad = jnp.zeros((n_tiles,), I32).at[pend // bm_e - 1].max((padded > counts).astype(I32))
    tile_fill = jnp.where(tiles < n_used, has_pad, 1)
    tf = _pick_block(exp_w2.shape[1], MOE_UP_CHUNK)
    tn = _pick_block(d, MOE_DOWN_CHUNK)
    work_up = _work_list(pstart // bm_e, padded // bm_e, n_used, n_tiles, exp_w2.shape[1] // tf)
    work_down = _work_list(pstart // bm_e, padded // bm_e, n_used, n_tiles, d // tn)

    xs = _dispatch(xp, dest_flat, tile_fill, bm_e)
    hid = _moe_up(xs, work_up, exp_w1, exp_b1, bm_e, tf)
    ys = _moe_down(hid, work_down, exp_w2, exp_b2, bm_e, tn)
    return _combine(x1, gate, dest_flat, ys, tn)


def kernel(x, mem, norm_mix, norm_ffn, norm_mem, w_in, b_gate, b_forget, q_norm_attn, k_norm_attn, q_norm_mem, k_norm_mem, w_mem_kv, lam_re, lam_im, log_step, ssm_b_re, ssm_b_im, ssm_c_re, ssm_c_im, ssm_d, w_glu, b_glu, w_up_attn, w_up_ssm, w_up_mem, w_out, w_router, b_router, exp_w1, exp_b1, exp_w2, exp_b2):
    b, s, d = x.shape
    assert b == 1, "one sequence per call"
    x2 = x.reshape(s, d)
    for l in range(norm_mix.shape[0]):
        x2 = _mixer(x2, mem.reshape(mem.shape[1], d), norm_mix[l], norm_mem[l], w_in[l], b_gate[l], b_forget[l],
                    q_norm_attn[l], k_norm_attn[l], q_norm_mem[l], k_norm_mem[l], w_mem_kv[l], lam_re[l], lam_im[l],
                    log_step[l], ssm_b_re[l], ssm_b_im[l], ssm_c_re[l], ssm_c_im[l], ssm_d[l], w_glu[l], b_glu[l],
                    w_up_attn[l], w_up_ssm[l], w_up_mem[l], w_out[l])
        x2 = _moe(x2, norm_ffn[l], w_router[l], b_router[l], exp_w1[l], exp_b1[l], exp_w2[l], exp_b2[l])
    return x2.reshape(b, s, d)
```

```python
import functools
import math

import jax
import jax.numpy as jnp
from jax import lax
from jax.experimental import pallas as pl
from jax.experimental.pallas import tpu as pltpu

F32 = jnp.float32
BF16 = jnp.bfloat16
I32 = jnp.int32
U32 = jnp.uint32

RMS_EPS = 1e-6
TOP_K = 4
SWIGLU_ALPHA = 1.702
SWIGLU_LIMIT = 7.0

LANES = 128
VMEM_LIMIT = 56 * 1024 * 1024
SSM_CHUNK = 8
MOE_UP_CHUNK = 384
MOE_DOWN_CHUNK = 2048
HIGHEST = lax.Precision.HIGHEST
LOG2E = math.log2(math.e)
EXP2_UNDERFLOW = 160.0


def _pick_block(n, target, align=LANES):
    if n <= target:
        return n
    b = target // align * align
    while n % b:
        b -= align
    return b


def _pack_bf16_pair(lo, hi):
    a = pltpu.bitcast(lo.astype(BF16).astype(F32), U32) >> 16
    b = pltpu.bitcast(hi.astype(BF16).astype(F32), U32) & jnp.uint32(0xFFFF0000)
    return a | b


def _unpack_bf16_pair(w):
    return pltpu.bitcast(w << 16, F32), pltpu.bitcast(w & jnp.uint32(0xFFFF0000), F32)


def _params(*sem):
    return pltpu.CompilerParams(dimension_semantics=sem, vmem_limit_bytes=VMEM_LIMIT)


def _rmsnorm_kernel(x_ref, g_ref, o_ref):
    x = x_ref[...]
    ms = jnp.mean(x * x, axis=-1, keepdims=True)
    o_ref[...] = (x * lax.rsqrt(ms + RMS_EPS) * g_ref[...]).astype(o_ref.dtype)


def _rmsnorm(x, g, out_dtype, bm=256):
    m, d = x.shape
    bm = min(bm, m)
    return pl.pallas_call(
        _rmsnorm_kernel,
        grid=(m // bm,),
        in_specs=[pl.BlockSpec((bm, d), lambda i: (i, 0)), pl.BlockSpec((1, d), lambda i: (0, 0))],
        out_specs=pl.BlockSpec((bm, d), lambda i: (i, 0)),
        out_shape=jax.ShapeDtypeStruct((m, d), out_dtype),
        compiler_params=_params("parallel"),
        name="rmsnorm",
    )(x, g.reshape(1, d))


def _mm_kernel(*refs, n_extra, epilogue, transpose_out):
    x_ref, w_ref = refs[0], refs[1]
    extras = refs[2:2 + n_extra]
    o_ref = refs[2 + n_extra]
    acc = jnp.dot(x_ref[...].astype(BF16), w_ref[...], preferred_element_type=F32)
    out = epilogue(acc, *[e[...] for e in extras])
    o_ref[...] = (out.T if transpose_out else out).astype(o_ref.dtype)


def _matmul(x, w, extras, epilogue, out_dtype, bm, bn, name, w_cols=None, transpose_out=False):
    m, k = x.shape
    c0, n = w_cols if w_cols is not None else (0, w.shape[1])
    bm, bn = _pick_block(m, bm, 8), _pick_block(n, bn)
    while c0 % bn or n % bn:
        bn -= LANES
    wj = c0 // bn
    in_specs = [pl.BlockSpec((bm, k), lambda i, j: (i, 0)), pl.BlockSpec((k, bn), lambda i, j: (0, j + wj))]
    args = [x, w]
    for arr, kind, off in extras:
        if kind == "row":
            in_specs.append(pl.BlockSpec((1, bn), lambda i, j, off=off: (0, j + off)))
        else:
            in_specs.append(pl.BlockSpec((bm, bn), lambda i, j, off=off: (i, j + off)))
        args.append(arr)
    if transpose_out:
        out_spec, out_shape = pl.BlockSpec((bn, bm), lambda i, j: (j, i)), (n, m)
    else:
        out_spec, out_shape = pl.BlockSpec((bm, bn), lambda i, j: (i, j)), (m, n)
    return pl.pallas_call(
        functools.partial(_mm_kernel, n_extra=len(extras), epilogue=epilogue, transpose_out=transpose_out),
        grid=(m // bm, n // bn),
        in_specs=in_specs,
        out_specs=out_spec,
        out_shape=jax.ShapeDtypeStruct(out_shape, out_dtype),
        compiler_params=_params("parallel", "parallel"),
        name=name,
    )(*args)


def _realign_kernel(w_ref, main_ref, f_ref, rest_ref, *, o_f, o_u, o_g, n_heads):
    w = w_ref[...]
    main_ref[...] = w[:, :o_f].astype(BF16)
    fcols = w[:, o_f:o_f + LANES]
    lane = lax.broadcasted_iota(I32, fcols.shape, 1)
    f_ref[...] = jnp.where(lane < n_heads, fcols, 0.0).astype(BF16)
    n_gate = w.shape[1] - o_g
    rest_ref[:, :n_gate] = w[:, o_g:].astype(BF16)
    rest_ref[:, n_gate:] = w[:, o_u:o_g].astype(BF16)


def _realign_w_in(w_in, o_f, o_u, o_g, n_heads, br=64):
    d, n = w_in.shape
    assert o_f % LANES == 0 and (n - o_g) % LANES == 0 and (o_g - o_u) % LANES == 0
    br = _pick_block(d, br, 8)
    n_rest = n - o_u
    return pl.pallas_call(
        functools.partial(_realign_kernel, o_f=o_f, o_u=o_u, o_g=o_g, n_heads=n_heads),
        grid=(d // br,),
        in_specs=[pl.BlockSpec((br, n), lambda i: (i, 0))],
        out_specs=[pl.BlockSpec((br, o_f), lambda i: (i, 0)), pl.BlockSpec((br, LANES), lambda i: (i, 0)),
                   pl.BlockSpec((br, n_rest), lambda i: (i, 0))],
        out_shape=[jax.ShapeDtypeStruct((d, o_f), BF16), jax.ShapeDtypeStruct((d, LANES), BF16),
                   jax.ShapeDtypeStruct((d, n_rest), BF16)],
        compiler_params=_params("parallel"),
        name="realign_w_in",
    )(w_in)


def _ep_plain(acc):
    return acc


def _ep_headnorm(acc, gain, *, dh, scale):
    outs = []
    for h in range(acc.shape[1] // dh):
        blk = acc[:, h * dh:(h + 1) * dh]
        ms = jnp.mean(blk * blk, axis=-1, keepdims=True)
        outs.append(blk * lax.rsqrt(ms + RMS_EPS) * gain[:, h * dh:(h + 1) * dh] * scale)
    return outs[0] if len(outs) == 1 else jnp.concatenate(outs, axis=1)


def _ep_sigmoid_bias(acc, b):
    return jax.nn.sigmoid(acc + b)


def _ep_glu(acc, b, y):
    return y * jax.nn.sigmoid(acc + b)


def _ep_residual(acc, r):
    return acc + r


def _split3(x):
    hi = x.astype(BF16)
    r1 = x - hi.astype(F32)
    mid = r1.astype(BF16)
    lo = (r1 - mid.astype(F32)).astype(BF16)
    return hi, mid, lo


def _forget_cumsum_kernel(f_ref, b_ref, c_ref, hi_ref, mid_ref, lo_ref, carry_ref, *, bs):
    @pl.when(pl.program_id(0) == 0)
    def _():
        carry_ref[...] = jnp.zeros_like(carry_ref)

    z = f_ref[...] + b_ref[...]
    logf = jnp.minimum(z, 0.0) - jnp.log1p(jnp.exp(-jnp.abs(z)))
    row = lax.broadcasted_iota(I32, (bs, bs), 0)
    col = lax.broadcasted_iota(I32, (bs, bs), 1)
    tri = jnp.where(col <= row, 1.0, 0.0).astype(BF16)
    hi, mid, lo = _split3(logf)
    cs = (jnp.dot(tri, hi, preferred_element_type=F32) + jnp.dot(tri, mid, preferred_element_type=F32)
          + jnp.dot(tri, lo, preferred_element_type=F32)) + carry_ref[...]
    carry_ref[...] = cs[bs - 1:bs, :]
    c2 = cs * LOG2E
    c_ref[...] = c2
    hi_ref[...], mid_ref[...], lo_ref[...] = _split3(c2)


def _forget_cumsum(f_logit, b_forget_pad, bs=256):
    s, w = f_logit.shape
    bs = min(bs, s)
    blk = lambda: pl.BlockSpec((bs, w), lambda i: (i, 0))
    return pl.pallas_call(
        functools.partial(_forget_cumsum_kernel, bs=bs),
        grid=(s // bs,),
        in_specs=[blk(), pl.BlockSpec((1, w), lambda i: (0, 0))],
        out_specs=[blk(), blk(), blk(), blk()],
        out_shape=[jax.ShapeDtypeStruct((s, w), F32)] + [jax.ShapeDtypeStruct((s, w), BF16)] * 3,
        scratch_shapes=[pltpu.VMEM((1, w), F32)],
        compiler_params=_params("arbitrary"),
        name="forget_cumsum",
    )(f_logit, b_forget_pad)


def _fox_kernel(jstart_ref, qt_ref, qb_ref, k_ref, kb_ref, vt_ref, o_ref, m_ref, l_ref, acc_ref, *, nb, blk):
    h, i = pl.program_id(0), pl.program_id(1)
    qt = jnp.concatenate([qt_ref[...], qb_ref[...]], axis=0)

    def rows(j):
        return pl.ds(pl.multiple_of(j * blk, blk), blk)

    def scores(j):
        kj = jnp.concatenate([k_ref[rows(j), :], kb_ref[rows(j), :]], axis=1)
        return jnp.dot(kj, qt, preferred_element_type=F32)

    st = scores(i)
    kpos = lax.broadcasted_iota(I32, st.shape, 0)
    qpos = lax.broadcasted_iota(I32, st.shape, 1)
    st = jnp.where(kpos <= qpos, st, -jnp.inf)
    m0 = jnp.max(st, axis=0, keepdims=True)
    p = jnp.exp2(st - m0)
    m_ref[...] = m0
    l_ref[...] = jnp.sum(p, axis=0, keepdims=True)
    acc_ref[...] = jnp.dot(vt_ref[:, rows(i)], p.astype(BF16), preferred_element_type=F32)

    def body(t, carry):
        j = i - 1 - t
        st = scores(j)
        m_old = m_ref[...]
        m_new = jnp.maximum(m_old, jnp.max(st, axis=0, keepdims=True))
        p = jnp.exp2(st - m_new)
        corr = jnp.exp2(m_old - m_new)
        l_ref[...] = l_ref[...] * corr + jnp.sum(p, axis=0, keepdims=True)
        acc_ref[...] = acc_ref[...] * corr + jnp.dot(vt_ref[:, rows(j)], p.astype(BF16), preferred_element_type=F32)
        m_ref[...] = m_new
        return carry

    lax.fori_loop(0, i - jstart_ref[h * nb + i], body, 0)
    o_ref[...] = (acc_ref[...] * (1.0 / l_ref[...])).T.astype(o_ref.dtype)


def _fox_attention(qt, k, vt, c2, c_pieces, logit_bound, n_heads, dh, blk=512):
    s = k.shape[0]
    blk = min(blk, s)
    nb = s // blk
    pieces = jnp.stack([jnp.transpose(p[:, :n_heads]) for p in c_pieces], axis=1)
    ones = jnp.ones_like(pieces)
    qb = jnp.pad(jnp.concatenate([pieces, ones], axis=1), ((0, 0), (0, dh - 6), (0, 0)))
    kb = jnp.pad(jnp.transpose(jnp.concatenate([ones, -pieces], axis=1), (0, 2, 1)),
                 ((0, 0), (0, 0), (0, dh - 6)))

    ct = jnp.transpose(c2[:, :n_heads]).reshape(n_heads, nb, blk)
    max_bias = ct[:, :, None, 0] - ct[:, None, :, blk - 1]
    ii = jnp.arange(nb)
    needed = jnp.logical_and(max_bias >= -(2.0 * logit_bound + EXP2_UNDERFLOW), ii[None, None, :] <= ii[None, :, None])
    jstart = (ii[None, :] + 1 - jnp.sum(needed, axis=-1)).astype(I32).reshape(n_heads * nb)

    return pl.pallas_call(
        functools.partial(_fox_kernel, nb=nb, blk=blk),
        grid_spec=pltpu.PrefetchScalarGridSpec(
            num_scalar_prefetch=1,
            grid=(n_heads, nb),
            in_specs=[
                pl.BlockSpec((dh, blk), lambda h, i, js: (h, i)),
                pl.BlockSpec((None, dh, blk), lambda h, i, js: (h, 0, i)),
                pl.BlockSpec((s, dh), lambda h, i, js: (0, h)),
                pl.BlockSpec((None, s, dh), lambda h, i, js: (h, 0, 0)),
                pl.BlockSpec((dh, s), lambda h, i, js: (h, 0)),
            ],
            out_specs=pl.BlockSpec((blk, dh), lambda h, i, js: (i, h)),
            scratch_shapes=[pltpu.VMEM((1, blk), F32), pltpu.VMEM((1, blk), F32), pltpu.VMEM((dh, blk), F32)],
        ),
        out_shape=jax.ShapeDtypeStruct((s, n_heads * dh), BF16),
        compiler_params=_params("parallel", "arbitrary"),
        name="fox_attention",
    )(jstart, qt, qb, k, kb, vt)


def _s5_params_kernel(lre_ref, lim_ref, ls_ref, bre_ref, bim_ref, ctre_ref, ctim_ref, cre_ref, cim_ref,
                      zre_ref, zim_ref, wre_ref, wim_ref, m_ref, alre_ref, alim_ref, *, chunk, n_ch):
    lre, lim = lre_ref[...], lim_ref[...]
    step = jnp.exp(ls_ref[...])
    ar, ai = lre * step, lim * step
    lc = chunk * n_ch
    tau = (lax.broadcasted_iota(I32, (1, 1, lc), 2) // n_ch).astype(F32)

    def cexp(t):
        mag = jnp.exp(ar * t)
        return mag * jnp.cos(ai * t), mag * jnp.sin(ai * t)

    e1r, e1i = cexp(1.0)
    nr, ni = e1r - 1.0, e1i
    den = lre * lre + lim * lim
    cr, ci = (nr * lre + ni * lim) / den, (ni * lre - nr * lim) / den
    bre, bim = bre_ref[...], bim_ref[...]
    bbr, bbi = cr * bre - ci * bim, cr * bim + ci * bre
    er, ei = cexp(tau)
    zr, zi = er * bbr - ei * bbi, er * bbi + ei * bbr
    zre_ref[...] = zr
    zim_ref[...] = zi
    e1r_t, e1i_t = cexp(tau + 1.0)
    ctre, ctim = ctre_ref[...], ctim_ref[...]
    wre_ref[...] = ctre * e1r_t - ctim * e1i_t
    wim_ref[...] = ctre * e1i_t + ctim * e1r_t
    dn = (((2,), (1,)), ((0,), (0,)))
    m_ref[...] = (lax.dot_general(cre_ref[...], zr, dn, precision=HIGHEST, preferred_element_type=F32)
                  - lax.dot_general(cim_ref[...], zi, dn, precision=HIGHEST, preferred_element_type=F32))
    alr, ali = cexp(float(chunk))
    alre_ref[...] = alr
    alim_ref[...] = ali


def _s5_operators(lam_re, lam_im, log_step, b_re, b_im, c_re, c_im, chunk):
    g, n = lam_re.shape
    n_ch = b_re.shape[-1]
    gs = LANES // n_ch
    assert g % gs == 0
    nsl, lc = g // gs, chunk * n_ch
    tile = lambda a: jnp.tile(a, (1, 1, chunk))
    ct_re, ct_im = jnp.transpose(c_re, (0, 2, 1)), jnp.transpose(c_im, (0, 2, 1))
    spec3 = lambda a, b: pl.BlockSpec((gs, a, b), lambda i: (i, 0, 0))
    zre, zim, wre, wim, mk, alre, alim = pl.pallas_call(
        functools.partial(_s5_params_kernel, chunk=chunk, n_ch=n_ch),
        grid=(nsl,),
        in_specs=[spec3(n, 1), spec3(n, 1), spec3(1, 1), spec3(n, lc), spec3(n, lc), spec3(n, lc), spec3(n, lc),
                  spec3(n_ch, n), spec3(n_ch, n)],
        out_specs=[spec3(n, lc), spec3(n, lc), spec3(n, lc), spec3(n, lc), spec3(n_ch, lc), spec3(n, 1), spec3(n, 1)],
        out_shape=[jax.ShapeDtypeStruct((g, n, lc), F32)] * 4 + [jax.ShapeDtypeStruct((g, n_ch, lc), F32)]
        + [jax.ShapeDtypeStruct((g, n, 1), F32)] * 2,
        compiler_params=_params("parallel"),
        name="s5_params",
    )(lam_re.reshape(g, n, 1), lam_im.reshape(g, n, 1), log_step.reshape(g, 1, 1), tile(b_re), tile(b_im),
      tile(ct_re), tile(ct_im), c_re, c_im)

    eye = jnp.eye(gs, dtype=F32)
    m5 = mk.reshape(nsl, gs, n_ch, chunk, n_ch)
    kk = jnp.transpose(m5, (0, 3, 1, 4, 2))
    kk = (kk[:, :, :, :, None, :] * eye[None, None, :, None, :, None]).reshape(nsl, chunk, LANES, LANES)
    d = jnp.arange(chunk)[None, :] - jnp.arange(chunk)[:, None]
    tm = jnp.where((d >= 0)[None, :, :, None, None], kk[:, jnp.clip(d, 0, chunk - 1)], 0.0)
    t_op = jnp.transpose(tm, (0, 1, 3, 2, 4)).reshape(nsl, chunk * LANES, chunk * LANES).astype(BF16)

    def inject(z):
        z5 = z.reshape(nsl, gs, n, chunk, n_ch)[:, :, :, ::-1, :]
        pd = jnp.transpose(z5, (0, 3, 1, 4, 2))
        return (pd[:, :, :, :, None, :] * eye[None, None, :, None, :, None]).reshape(nsl, chunk * LANES, gs * n)

    def readout(w):
        w5 = w.reshape(nsl, gs, n, chunk, n_ch)
        return (w5[:, :, :, :, None, :] * eye[None, :, None, None, :, None]).reshape(nsl, gs * n, chunk * LANES)

    p_op = jnp.concatenate([inject(zre), inject(zim)], axis=-1).astype(BF16)
    q_op = jnp.concatenate([readout(wre), -readout(wim)], axis=1).astype(BF16)
    a_l = jnp.concatenate([alre.reshape(nsl, 1, gs * n), alim.reshape(nsl, 1, gs * n)], axis=-1)
    return t_op, p_op, q_op, a_l


def _gelu_tanh(x):
    return 0.5 * x * (1.0 + jnp.tanh(math.sqrt(2.0 / math.pi) * (x + 0.044715 * (x * x * x))))


def _s5_conv_kernel(u_ref, t_ref, p_ref, q_ref, a_ref, d_ref, o_ref, hst_ref, s_ref, hp_ref, *, chunk, bk):
    hn = a_ref.shape[1] // 2

    @pl.when(pl.program_id(1) == 0)
    def _():
        hst_ref[...] = jnp.zeros_like(hst_ref)

    x = jnp.concatenate([u_ref[pl.ds(s, bk, stride=chunk), :] for s in range(chunk)], axis=1).astype(BF16)
    s_ref[...] = jnp.dot(x, p_ref[...], preferred_element_type=F32)
    ar, ai = a_ref[:, :hn], a_ref[:, hn:]

    def body(k, h):
        hr, hi = h
        hp_ref[pl.ds(k, 1), :hn] = hr
        hp_ref[pl.ds(k, 1), hn:] = hi
        row = s_ref[pl.ds(k, 1), :]
        return ar * hr - ai * hi + row[:, :hn], ar * hi + ai * hr + row[:, hn:]

    hr, hi = lax.fori_loop(0, bk, body, (hst_ref[:, :hn], hst_ref[:, hn:]))
    hst_ref[:, :hn] = hr
    hst_ref[:, hn:] = hi

    y = (jnp.dot(x, t_ref[...], preferred_element_type=F32)
         + jnp.dot(hp_ref[...].astype(BF16), q_ref[...], preferred_element_type=F32))
    d = d_ref[...]
    for t in range(chunk):
        yt = y[:, t * LANES:(t + 1) * LANES] + d * u_ref[pl.ds(t, bk, stride=chunk), :]
        o_ref[pl.ds(t, bk, stride=chunk), :] = _gelu_tanh(yt)


def _s5_conv(u, t_op, p_op, q_op, a_l, d_skip, chunk, bk=512):
    s, p = u.shape
    nsl = p // LANES
    bk = min(bk, s // chunk)
    bm = bk * chunk
    hn2 = a_l.shape[-1]
    return pl.pallas_call(
        functools.partial(_s5_conv_kernel, chunk=chunk, bk=bk),
        grid=(nsl, s // bm),
        in_specs=[
            pl.BlockSpec((bm, LANES), lambda sl, i: (i, sl)),
            pl.BlockSpec((None, chunk * LANES, chunk * LANES), lambda sl, i: (sl, 0, 0)),
            pl.BlockSpec((None, chunk * LANES, hn2), lambda sl, i: (sl, 0, 0)),
            pl.BlockSpec((None, hn2, chunk * LANES), lambda sl, i: (sl, 0, 0)),
            pl.BlockSpec((None, 1, hn2), lambda sl, i: (sl, 0, 0)),
            pl.BlockSpec((1, LANES), lambda sl, i: (0, sl)),
        ],
        out_specs=pl.BlockSpec((bm, LANES), lambda sl, i: (i, sl)),
        out_shape=jax.ShapeDtypeStruct((s, p), F32),
        scratch_shapes=[pltpu.VMEM((1, hn2), F32), pltpu.VMEM((bk, hn2), F32), pltpu.VMEM((bk, hn2), F32)],
        compiler_params=_params("parallel", "arbitrary"),
        name="s5_conv",
    )(u, t_op, p_op, q_op, a_l, d_skip.reshape(1, p))


def _mem_attn_kernel(q_ref, k_ref, v_ref, o_ref, *, n_heads, dm):
    for h in range(n_heads):
        sl = slice(h * dm, (h + 1) * dm)
        s = lax.dot_general(q_ref[:, sl], k_ref[:, sl], (((1,), (1,)), ((), ())), preferred_element_type=F32)
        p = jnp.exp(s - jnp.max(s, axis=-1, keepdims=True))
        o = jnp.dot(p.astype(BF16), v_ref[:, sl], preferred_element_type=F32) / jnp.sum(p, axis=-1, keepdims=True)
        o_ref[:, sl] = o.astype(o_ref.dtype)


def _mem_attention(q, k, v, n_heads, dm, bm=512):
    s, w = q.shape
    m = k.shape[0]
    bm = min(bm, s)
    return pl.pallas_call(
        functools.partial(_mem_attn_kernel, n_heads=n_heads, dm=dm),
        grid=(s // bm,),
        in_specs=[pl.BlockSpec((bm, w), lambda i: (i, 0)), pl.BlockSpec((m, w), lambda i: (0, 0)),
                  pl.BlockSpec((m, w), lambda i: (0, 0))],
        out_specs=pl.BlockSpec((bm, w), lambda i: (i, 0)),
        out_shape=jax.ShapeDtypeStruct((s, w), BF16),
        compiler_params=_params("parallel"),
        name="mem_attention",
    )(q, k, v)


def _merge_kernel(ya_ref, ys_ref, ym_ref, wa_ref, ws_ref, wm_ref, ga_ref, gs_ref, gm_ref, o_ref):
    a = jnp.dot(ya_ref[...], wa_ref[...], preferred_element_type=F32)
    s = jnp.dot(ys_ref[...], ws_ref[...], preferred_element_type=F32)
    m = jnp.dot(ym_ref[...], wm_ref[...], preferred_element_type=F32)
    o_ref[...] = (ga_ref[...].astype(F32) * a + gs_ref[...].astype(F32) * s
                  + gm_ref[...].astype(F32) * m).astype(o_ref.dtype)


def _merge(ya, ys, ym, wa, ws, wm, gates, bm=1024, bn=512):
    s, d = ya.shape[0], wa.shape[1]
    bm, bn = min(bm, s), min(bn, d)
    nb = d // bn
    row = lambda a: pl.BlockSpec((bm, a.shape[1]), lambda i, j: (i, 0))
    col = lambda a: pl.BlockSpec((a.shape[0], bn), lambda i, j: (0, j))
    gate = lambda b: pl.BlockSpec((bm, bn), lambda i, j, b=b: (i, j + b * nb))
    return pl.pallas_call(
        _merge_kernel,
        grid=(s // bm, nb),
        in_specs=[row(ya), row(ys), row(ym), col(wa), col(ws), col(wm), gate(0), gate(1), gate(2)],
        out_specs=pl.BlockSpec((bm, bn), lambda i, j: (i, j)),
        out_shape=jax.ShapeDtypeStruct((s, d), BF16),
        compiler_params=_params("parallel", "parallel"),
        name="merge",
    )(ya, ys, ym, wa, ws, wm, gates, gates, gates)


def _router_kernel(x_ref, g_ref, wr_ref, br_ref, xp_ref, idx_ref, gate_ref, rank_ref, cnt_ref, carry_ref, *, bm):
    @pl.when(pl.program_id(0) == 0)
    def _():
        carry_ref[...] = jnp.zeros_like(carry_ref)

    x = x_ref[...]
    ms = jnp.mean(x * x, axis=-1, keepdims=True)
    xn = x * lax.rsqrt(ms + RMS_EPS) * g_ref[...]
    d2 = xn.shape[1] // 2
    xp_ref[...] = _pack_bf16_pair(xn[:, :d2], xn[:, d2:])

    n_exp = wr_ref.shape[1]
    logits = jnp.dot(xn, wr_ref[...], precision=HIGHEST, preferred_element_type=F32) + br_ref[...]
    lane = lax.broadcasted_iota(I32, logits.shape, 1)
    kcol = lax.broadcasted_iota(I32, (bm, TOP_K), 1)
    work = logits
    member = jnp.zeros_like(logits)
    idxs, vals = [], []
    for _ in range(TOP_K):
        mx = jnp.max(work, axis=-1, keepdims=True)
        ik = jnp.min(jnp.where(work == mx, lane, n_exp), axis=-1, keepdims=True)
        sel = lane == ik
        member = jnp.where(sel, 1.0, member)
        work = jnp.where(sel, -jnp.inf, work)
        idxs.append(ik)
        vals.append(mx)
    es = [jnp.exp(v - vals[0]) for v in vals]
    den = es[0] + es[1] + es[2] + es[3]

    r = lax.broadcasted_iota(I32, (bm, bm), 0)
    c = lax.broadcasted_iota(I32, (bm, bm), 1)
    tri = jnp.where(c < r, 1.0, 0.0).astype(BF16)
    prefix = jnp.dot(tri, member.astype(BF16), preferred_element_type=F32) + carry_ref[...]
    total = carry_ref[...] + jnp.sum(member, axis=0, keepdims=True)
    carry_ref[...] = total
    cnt_ref[...] = total

    idx_o = jnp.zeros((bm, TOP_K), I32)
    gate_o = jnp.zeros((bm, TOP_K), F32)
    rank_o = jnp.zeros((bm, TOP_K), F32)
    for k in range(TOP_K):
        rk = jnp.sum(jnp.where(lane == idxs[k], prefix, 0.0), axis=-1, keepdims=True)
        idx_o = jnp.where(kcol == k, idxs[k], idx_o)
        gate_o = jnp.where(kcol == k, es[k] / den, gate_o)
        rank_o = jnp.where(kcol == k, rk, rank_o)
    idx_ref[...] = idx_o
    gate_ref[...] = gate_o
    rank_ref[...] = rank_o.astype(I32)


def _router(x1, g, w_router, b_router, bm=256):
    t, d = x1.shape
    n_exp = w_router.shape[1]
    bm = min(bm, t)
    small = lambda: pl.BlockSpec((bm, TOP_K), lambda i: (i, 0))
    return pl.pallas_call(
        functools.partial(_router_kernel, bm=bm),
        grid=(t // bm,),
        in_specs=[pl.BlockSpec((bm, d), lambda i: (i, 0)), pl.BlockSpec((1, d), lambda i: (0, 0)),
                  pl.BlockSpec((d, n_exp), lambda i: (0, 0)), pl.BlockSpec((1, n_exp), lambda i: (0, 0))],
        out_specs=[pl.BlockSpec((bm, d // 2), lambda i: (i, 0)), small(), small(), small(),
                   pl.BlockSpec((1, n_exp), lambda i: (0, 0))],
        out_shape=[jax.ShapeDtypeStruct((t, d // 2), U32), jax.ShapeDtypeStruct((t, TOP_K), I32),
                   jax.ShapeDtypeStruct((t, TOP_K), F32), jax.ShapeDtypeStruct((t, TOP_K), I32),
                   jax.ShapeDtypeStruct((1, n_exp), F32)],
        scratch_shapes=[pltpu.VMEM((1, n_exp), F32)],
        compiler_params=_params("arbitrary"),
        name="router",
    )(x1, g.reshape(1, d), w_router, b_router.reshape(1, n_exp))


def _dispatch_kernel(fill_ref, dest_ref, xp_ref, xs_ref, zero_ref, sem_z, sem_r, *, bm, bm_e, n_tiles):
    @pl.when(pl.program_id(0) == 0)
    def _():
        zero_ref[...] = jnp.zeros_like(zero_ref)

        def fill(t, wait):
            @pl.when(fill_ref[t] > 0)
            def _():
                cp = pltpu.make_async_copy(zero_ref, xs_ref.at[pl.ds(t * bm_e, bm_e), :], sem_z)
                cp.wait() if wait else cp.start()

        lax.fori_loop(0, n_tiles, lambda t, c: (fill(t, False), c)[1], 0)
        lax.fori_loop(0, n_tiles, lambda t, c: (fill(t, True), c)[1], 0)

    def row_copy(r, k):
        return pltpu.make_async_copy(xp_ref.at[pl.ds(r, 1), :], xs_ref.at[pl.ds(dest_ref[r * TOP_K + k], 1), :], sem_r)

    def issue(r, c):
        for k in range(TOP_K):
            row_copy(r, k).start()
        return c

    def drain(r, c):
        for k in range(TOP_K):
            row_copy(r, k).wait()
        return c

    lax.fori_loop(0, bm, issue, 0)
    lax.fori_loop(0, bm, drain, 0)


def _dispatch(xp, dest_flat, tile_fill, bm_e, bm=256):
    t, d2 = xp.shape
    bm = min(bm, t)
    n_tiles = tile_fill.shape[0]
    n_rows = n_tiles * bm_e
    return pl.pallas_call(
        functools.partial(_dispatch_kernel, bm=bm, bm_e=bm_e, n_tiles=n_tiles),
        grid_spec=pltpu.PrefetchScalarGridSpec(
            num_scalar_prefetch=1,
            grid=(t // bm,),
            in_specs=[pl.BlockSpec((bm * TOP_K,), lambda i, lt: (i,), memory_space=pltpu.SMEM),
                      pl.BlockSpec((bm, d2), lambda i, lt: (i, 0))],
            out_specs=pl.BlockSpec(memory_space=pl.ANY),
            scratch_shapes=[pltpu.VMEM((bm_e, d2), U32), pltpu.SemaphoreType.DMA(()), pltpu.SemaphoreType.DMA(())],
        ),
        out_shape=jax.ShapeDtypeStruct((n_rows, d2), U32),
        compiler_params=_params("arbitrary"),
        name="moe_dispatch",
    )(tile_fill, dest_flat, xp)


FIRST_VISIT, TILE_USED = 1, 2


def _work_list(tile_start, tile_count, n_used, n_tiles, n_chunks):
    n_exp = tile_start.shape[0]
    w = jnp.arange(n_tiles * n_chunks, dtype=I32)
    e = jnp.minimum(jnp.sum((n_chunks * (tile_start + tile_count))[None, :] <= w[:, None], axis=1), n_exp - 1).astype(I32)
    cnt = jnp.maximum(tile_count[e], 1)
    r = w - n_chunks * tile_start[e]
    used = w < n_chunks * n_used
    n_unused = jnp.maximum(n_tiles - n_used, 1)
    r2 = w - n_chunks * n_used
    tile = jnp.where(used, tile_start[e] + r % cnt, n_used + r2 % n_unused)
    chunk = jnp.where(used, r // cnt, r2 // n_unused)
    last = n_chunks * n_used - 1
    in_tile = jnp.where(used, tile, tile[last])
    w_exp = jnp.where(used, e, e[last])
    w_chunk = jnp.where(used, chunk, chunk[last])
    flags = jnp.where(used, TILE_USED + jnp.where(r % cnt == 0, FIRST_VISIT, 0), 0)
    return [a.astype(I32) for a in (tile, chunk, in_tile, w_exp, w_chunk, flags)]


def _moe_up_kernel(tile_ref, chunk_ref, xin_ref, we_ref, wc_ref, flag_ref, xs_ref, wg_ref, wl_ref, bg_ref, bl_ref,
                   o_ref, wb_ref):
    flag = flag_ref[pl.program_id(0)]
    tf = wg_ref.shape[1]

    @pl.when((flag & FIRST_VISIT) != 0)
    def _():
        wb_ref[:, :tf] = wg_ref[...].astype(BF16)
        wb_ref[:, tf:] = wl_ref[...].astype(BF16)

    @pl.when((flag & TILE_USED) != 0)
    def _():
        lo, hi = _unpack_bf16_pair(xs_ref[...])
        xlo, xhi = lo.astype(BF16), hi.astype(BF16)
        d2 = xlo.shape[1]
        hcat = (jnp.dot(xlo, wb_ref[:d2, :], preferred_element_type=F32)
                + jnp.dot(xhi, wb_ref[d2:, :], preferred_element_type=F32))
        glu = jnp.minimum(hcat[:, :tf] + bg_ref[...], SWIGLU_LIMIT)
        lin = jnp.clip(hcat[:, tf:] + bl_ref[...], -SWIGLU_LIMIT, SWIGLU_LIMIT)
        o_ref[...] = (glu * jax.nn.sigmoid(SWIGLU_ALPHA * glu) * (lin + 1.0)).astype(o_ref.dtype)

    @pl.when((flag & TILE_USED) == 0)
    def _():
        o_ref[...] = jnp.zeros_like(o_ref)


def _moe_up(xs, work, w1, b1, bm_e, tf):
    n_rows, d2 = xs.shape
    n_exp, d, f2 = w1.shape
    ff = f2 // 2
    nf = ff // tf
    wspec = lambda off: pl.BlockSpec((None, d, tf), lambda w, t, c, xi, we, wc, fl: (we[w], 0, wc[w] + off))
    bspec = lambda off: pl.BlockSpec((None, 1, tf), lambda w, t, c, xi, we, wc, fl: (we[w], 0, wc[w] + off))
    return pl.pallas_call(
        _moe_up_kernel,
        grid_spec=pltpu.PrefetchScalarGridSpec(
            num_scalar_prefetch=6,
            grid=(work[0].shape[0],),
            in_specs=[pl.BlockSpec((bm_e, d2), lambda w, t, c, xi, we, wc, fl: (xi[w], 0)),
                      wspec(0), wspec(nf), bspec(0), bspec(nf)],
            out_specs=pl.BlockSpec((bm_e, tf), lambda w, t, c, xi, we, wc, fl: (t[w], c[w])),
            scratch_shapes=[pltpu.VMEM((d, 2 * tf), BF16)],
        ),
        out_shape=jax.ShapeDtypeStruct((n_rows, ff), BF16),
        compiler_params=_params("arbitrary"),
        name="moe_up",
    )(*work, xs, w1, w1, b1.reshape(n_exp, 1, f2), b1.reshape(n_exp, 1, f2))


def _moe_down_kernel(tile_ref, chunk_ref, hin_ref, we_ref, wc_ref, flag_ref, h_ref, w2_ref, b2_ref, o_ref, w2b_ref):
    flag = flag_ref[pl.program_id(0)]

    @pl.when((flag & FIRST_VISIT) != 0)
    def _():
        w2b_ref[...] = w2_ref[...].astype(BF16)

    @pl.when((flag & TILE_USED) != 0)
    def _():
        y = jnp.dot(h_ref[...], w2b_ref[...], preferred_element_type=F32) + b2_ref[...]
        half = y.shape[1] // 2
        o_ref[...] = _pack_bf16_pair(y[:, :half], y[:, half:])

    @pl.when((flag & TILE_USED) == 0)
    def _():
        o_ref[...] = jnp.zeros_like(o_ref)


def _moe_down(hid, work, w2, b2, bm_e, tn):
    n_rows, ff = hid.shape
    n_exp, _, d = w2.shape
    return pl.pallas_call(
        _moe_down_kernel,
        grid_spec=pltpu.PrefetchScalarGridSpec(
            num_scalar_prefetch=6,
            grid=(work[0].shape[0],),
            in_specs=[pl.BlockSpec((bm_e, ff), lambda w, t, c, hi, we, wc, fl: (hi[w], 0)),
                      pl.BlockSpec((None, ff, tn), lambda w, t, c, hi, we, wc, fl: (we[w], 0, wc[w])),
                      pl.BlockSpec((None, 1, tn), lambda w, t, c, hi, we, wc, fl: (we[w], 0, wc[w]))],
            out_specs=pl.BlockSpec((bm_e, tn // 2), lambda w, t, c, hi, we, wc, fl: (t[w], c[w])),
            scratch_shapes=[pltpu.VMEM((ff, tn), BF16)],
        ),
        out_shape=jax.ShapeDtypeStruct((n_rows, d // 2), U32),
        compiler_params=_params("arbitrary"),
        name="moe_down",
    )(*work, hid, w2, b2.reshape(n_exp, 1, d))


def _combine_kernel(dest_ref, x_ref, gate_ref, ys_ref, o_ref, buf_ref, sem, *, bm, tn):
    def row_copy(r, k):
        return pltpu.make_async_copy(ys_ref.at[pl.ds(dest_ref[r * TOP_K + k], 1), :],
                                     buf_ref.at[k, pl.ds(r, 1), :], sem)

    def issue(r, c):
        for k in range(TOP_K):
            row_copy(r, k).start()
        return c

    def drain(r, c):
        for k in range(TOP_K):
            row_copy(r, k).wait()
        return c

    lax.fori_loop(0, bm, issue, 0)
    lax.fori_loop(0, bm, drain, 0)
    g = gate_ref[...]
    half = tn // 2
    for n in range(x_ref.shape[1] // tn):
        acc_lo = x_ref[:, n * tn:n * tn + half]
        acc_hi = x_ref[:, n * tn + half:(n + 1) * tn]
        for k in range(TOP_K):
            lo, hi = _unpack_bf16_pair(buf_ref[k, :, n * half:(n + 1) * half])
            acc_lo = acc_lo + g[:, k:k + 1] * lo
            acc_hi = acc_hi + g[:, k:k + 1] * hi
        o_ref[:, n * tn:n * tn + half] = acc_lo
        o_ref[:, n * tn + half:(n + 1) * tn] = acc_hi


def _combine(x1, gate, dest_flat, ys, tn, bm=128):
    t, d = x1.shape
    bm = min(bm, t)
    return pl.pallas_call(
        functools.partial(_combine_kernel, bm=bm, tn=tn),
        grid=(t // bm,),
        in_specs=[pl.BlockSpec((bm * TOP_K,), lambda i: (i,), memory_space=pltpu.SMEM),
                  pl.BlockSpec((bm, d), lambda i: (i, 0)),
                  pl.BlockSpec((bm, TOP_K), lambda i: (i, 0)),
                  pl.BlockSpec(memory_space=pl.ANY)],
        out_specs=pl.BlockSpec((bm, d), lambda i: (i, 0)),
        out_shape=jax.ShapeDtypeStruct((t, d), F32),
        scratch_shapes=[pltpu.VMEM((TOP_K, bm, d // 2), U32), pltpu.SemaphoreType.DMA(())],
        compiler_params=_params("arbitrary"),
        name="moe_combine",
    )(dest_flat, x1, gate, ys)


def _mixer(x2, mem2, norm_mix, norm_mem, w_in, b_gate, b_forget, q_norm_attn, k_norm_attn, q_norm_mem, k_norm_mem,
           w_mem_kv, lam_re, lam_im, log_step, ssm_b_re, ssm_b_im, ssm_c_re, ssm_c_im, ssm_d, w_glu, b_glu,
           w_up_attn, w_up_ssm, w_up_mem, w_out):
    s, d = x2.shape
    n_heads, dh = b_forget.shape[0], q_norm_attn.shape[0]
    aw = n_heads * dh
    pw = ssm_d.shape[0]
    dm = q_norm_mem.shape[0]
    mw = w_up_mem.shape[0]
    mem_heads = mw // dm
    o_k, o_v, o_f, o_u, o_qm, o_g = aw, 2 * aw, 3 * aw, 3 * aw + n_heads, 3 * aw + n_heads + pw, 3 * aw + n_heads + pw + mw
    bf = lambda a: a.astype(BF16)

    h = _rmsnorm(x2, norm_mix, BF16)
    w_qkv, w_f, w_rest = _realign_w_in(w_in, o_f, o_u, o_g, n_heads)
    qt = _matmul(h, w_qkv, [(jnp.tile(q_norm_attn, n_heads).reshape(1, aw), "row", 0)],
                 functools.partial(_ep_headnorm, dh=dh, scale=dh ** -0.5 * LOG2E), BF16, 1024, 512, "in_q",
                 w_cols=(0, aw), transpose_out=True)
    k = _matmul(h, w_qkv, [(jnp.tile(k_norm_attn, n_heads).reshape(1, aw), "row", 0)],
                functools.partial(_ep_headnorm, dh=dh, scale=1.0), BF16, 1024, 512, "in_k", w_cols=(o_k, aw))
    vt = _matmul(h, w_qkv, [], _ep_plain, BF16, 1024, 512, "in_v", w_cols=(o_v, aw), transpose_out=True)
    f_logit = _matmul(h, w_f, [], _ep_plain, F32, 1024, LANES, "in_f")
    gates = _matmul(h, w_rest, [(b_gate.reshape(1, 3 * d), "row", 0)], _ep_sigmoid_bias, BF16, 1024, 1024,
                    "in_gates", w_cols=(0, 3 * d))
    u = _matmul(h, w_rest, [], _ep_plain, F32, 1024, 512, "in_u", w_cols=(3 * d, pw))
    qm = _matmul(h, w_rest, [(jnp.tile(q_norm_mem, mem_heads).reshape(1, mw), "row", 0)],
                 functools.partial(_ep_headnorm, dh=dm, scale=dm ** -0.5), BF16, 1024, 512, "in_qm",
                 w_cols=(3 * d + pw, mw))

    c2, *c_pieces = _forget_cumsum(f_logit, jnp.pad(b_forget, (0, LANES - n_heads)).reshape(1, LANES))
    logit_bound = 1.02 * LOG2E * dh ** 0.5 * jnp.max(jnp.abs(q_norm_attn)) * jnp.max(jnp.abs(k_norm_attn))
    y_attn = _fox_attention(qt, k, vt, c2, c_pieces, logit_bound, n_heads, dh)

    t_op, p_op, q_op, a_l = _s5_operators(lam_re, lam_im, log_step, ssm_b_re, ssm_b_im, ssm_c_re, ssm_c_im, SSM_CHUNK)
    yg = _s5_conv(u, t_op, p_op, q_op, a_l, ssm_d, SSM_CHUNK)
    y_ssm = _matmul(yg, bf(w_glu), [(b_glu.reshape(1, pw), "row", 0), (yg, "tile", 0)], _ep_glu, BF16, 1024, 512, "s5_glu")

    mem_h = _rmsnorm(mem2, norm_mem, BF16)
    k_m = _matmul(mem_h, bf(w_mem_kv[:, :mw]), [(jnp.tile(k_norm_mem, mem_heads).reshape(1, mw), "row", 0)],
                  functools.partial(_ep_headnorm, dh=dm, scale=1.0), BF16, 256, 512, "mem_k")
    v_m = _matmul(mem_h, bf(w_mem_kv[:, mw:]), [], _ep_plain, BF16, 256, 512, "mem_v")
    y_mem = _mem_attention(qm, k_m, v_m, mem_heads, dm)

    merged = _merge(y_attn, y_ssm, y_mem, bf(w_up_attn), bf(w_up_ssm), bf(w_up_mem), gates)
    return _matmul(merged, bf(w_out), [(x2, "tile", 0)], _ep_residual, F32, 1024, 512, "out_proj")


def _moe(x1, norm_ffn, w_router, b_router, exp_w1, exp_b1, exp_w2, exp_b2, bm_e=512):
    t, d = x1.shape
    n_exp = w_router.shape[1]
    xp, idx, gate, rank, counts = _router(x1, norm_ffn, w_router, b_router)

    counts = counts.reshape(n_exp).astype(I32)
    padded = (counts + bm_e - 1) // bm_e * bm_e
    pend = jnp.cumsum(padded)
    pstart = pend - padded
    dest_flat = (pstart[idx] + rank).reshape(t * TOP_K)
    n_tiles = -(-(t * TOP_K) // bm_e) + n_exp
    n_used = (pend[-1] // bm_e).astype(I32)
    tiles = jnp.arange(n_tiles, dtype=I32)
    has_pad = jnp.zeros((n_tiles,), I32).at[pend // bm_e - 1].max((padded > counts).astype(I32))
    tile_fill = jnp.where(tiles < n_used, has_pad, 1)
    tf = _pick_block(exp_w2.shape[1], MOE_UP_CHUNK)
    tn = _pick_block(d, MOE_DOWN_CHUNK)
    work_up = _work_list(pstart // bm_e, padded // bm_e, n_used, n_tiles, exp_w2.shape[1] // tf)
    work_down = _work_list(pstart // bm_e, padded // bm_e, n_used, n_tiles, d // tn)

    xs = _dispatch(xp, dest_flat, tile_fill, bm_e)
    hid = _moe_up(xs, work_up, exp_w1, exp_b1, bm_e, tf)
    ys = _moe_down(hid, work_down, exp_w2, exp_b2, bm_e, tn)
    return _combine(x1, gate, dest_flat, ys, tn)


def kernel(x, mem, norm_mix, norm_ffn, norm_mem, w_in, b_gate, b_forget, q_norm_attn, k_norm_attn, q_norm_mem, k_norm_mem, w_mem_kv, lam_re, lam_im, log_step, ssm_b_re, ssm_b_im, ssm_c_re, ssm_c_im, ssm_d, w_glu, b_glu, w_up_attn, w_up_ssm, w_up_mem, w_out, w_router, b_router, exp_w1, exp_b1, exp_w2, exp_b2):
    b, s, d = x.shape
    assert b == 1, "one sequence per call"
    x2 = x.reshape(s, d)
    for l in range(norm_mix.shape[0]):
        x2 = _mixer(x2, mem.reshape(mem.shape[1], d), norm_mix[l], norm_mem[l], w_in[l], b_gate[l], b_forget[l],
                    q_norm_attn[l], k_norm_attn[l], q_norm_mem[l], k_norm_mem[l], w_mem_kv[l], lam_re[l], lam_im[l],
                    log_step[l], ssm_b_re[l], ssm_b_im[l], ssm_c_re[l], ssm_c_im[l], ssm_d[l], w_glu[l], b_glu[l],
                    w_up_attn[l], w_up_ssm[l], w_up_mem[l], w_out[l])
        x2 = _moe(x2, norm_ffn[l], w_router[l], b_router[l], exp_w1[l], exp_b1[l], exp_w2[l], exp_b2[l])
    return x2.reshape(b, s, d)
```

```python
import functools
import math

import jax
import jax.numpy as jnp
from jax import lax
from jax.experimental import pallas as pl
from jax.experimental.pallas import tpu as pltpu

F32 = jnp.float32
BF16 = jnp.bfloat16
I32 = jnp.int32
U32 = jnp.uint32

RMS_EPS = 1e-6
TOP_K = 4
SWIGLU_ALPHA = 1.702
SWIGLU_LIMIT = 7.0

LANES = 128
VMEM_LIMIT = 56 * 1024 * 1024
SSM_CHUNK = 8
MOE_UP_CHUNK = 384
MOE_DOWN_CHUNK = 2048
HIGHEST = lax.Precision.HIGHEST
LOG2E = math.log2(math.e)
EXP2_UNDERFLOW = 160.0


def _pick_block(n, target, align=LANES):
    if n <= target:
        return n
    b = target // align * align
    while n % b:
        b -= align
    return b


def _pack_bf16_pair(lo, hi):
    a = pltpu.bitcast(lo.astype(BF16).astype(F32), U32) >> 16
    b = pltpu.bitcast(hi.astype(BF16).astype(F32), U32) & jnp.uint32(0xFFFF0000)
    return a | b


def _unpack_bf16_pair(w):
    return pltpu.bitcast(w << 16, F32), pltpu.bitcast(w & jnp.uint32(0xFFFF0000), F32)


def _params(*sem):
    return pltpu.CompilerParams(dimension_semantics=sem, vmem_limit_bytes=VMEM_LIMIT)


def _rmsnorm_kernel(x_ref, g_ref, o_ref):
    x = x_ref[...]
    ms = jnp.mean(x * x, axis=-1, keepdims=True)
    o_ref[...] = (x * lax.rsqrt(ms + RMS_EPS) * g_ref[...]).astype(o_ref.dtype)


def _rmsnorm(x, g, out_dtype, bm=256):
    m, d = x.shape
    bm = min(bm, m)
    return pl.pallas_call(
        _rmsnorm_kernel,
        grid=(m // bm,),
        in_specs=[pl.BlockSpec((bm, d), lambda i: (i, 0)), pl.BlockSpec((1, d), lambda i: (0, 0))],
        out_specs=pl.BlockSpec((bm, d), lambda i: (i, 0)),
        out_shape=jax.ShapeDtypeStruct((m, d), out_dtype),
        compiler_params=_params("parallel"),
        name="rmsnorm",
    )(x, g.reshape(1, d))


def _mm_kernel(*refs, n_extra, epilogue, transpose_out):
    x_ref, w_ref = refs[0], refs[1]
    extras = refs[2:2 + n_extra]
    o_ref = refs[2 + n_extra]
    acc = jnp.dot(x_ref[...].astype(BF16), w_ref[...], preferred_element_type=F32)
    out = epilogue(acc, *[e[...] for e in extras])
    o_ref[...] = (out.T if transpose_out else out).astype(o_ref.dtype)


def _matmul(x, w, extras, epilogue, out_dtype, bm, bn, name, w_cols=None, transpose_out=False):
    m, k = x.shape
    c0, n = w_cols if w_cols is not None else (0, w.shape[1])
    bm, bn = _pick_block(m, bm, 8), _pick_block(n, bn)
    while c0 % bn or n % bn:
        bn -= LANES
    wj = c0 // bn
    in_specs = [pl.BlockSpec((bm, k), lambda i, j: (i, 0)), pl.BlockSpec((k, bn), lambda i, j: (0, j + wj))]
    args = [x, w]
    for arr, kind, off in extras:
        if kind == "row":
            in_specs.append(pl.BlockSpec((1, bn), lambda i, j, off=off: (0, j + off)))
        else:
            in_specs.append(pl.BlockSpec((bm, bn), lambda i, j, off=off: (i, j + off)))
        args.append(arr)
    if transpose_out:
        out_spec, out_shape = pl.BlockSpec((bn, bm), lambda i, j: (j, i)), (n, m)
    else:
        out_spec, out_shape = pl.BlockSpec((bm, bn), lambda i, j: (i, j)), (m, n)
    return pl.pallas_call(
        functools.partial(_mm_kernel, n_extra=len(extras), epilogue=epilogue, transpose_out=transpose_out),
        grid=(m // bm, n // bn),
        in_specs=in_specs,
        out_specs=out_spec,
        out_shape=jax.ShapeDtypeStruct(out_shape, out_dtype),
        compiler_params=_params("parallel", "parallel"),
        name=name,
    )(*args)


def _realign_kernel(w_ref, main_ref, f_ref, rest_ref, *, o_f, o_u, o_g, n_heads):
    w = w_ref[...]
    main_ref[...] = w[:, :o_f].astype(BF16)
    fcols = w[:, o_f:o_f + LANES]
    lane = lax.broadcasted_iota(I32, fcols.shape, 1)
    f_ref[...] = jnp.where(lane < n_heads, fcols, 0.0).astype(BF16)
    n_gate = w.shape[1] - o_g
    rest_ref[:, :n_gate] = w[:, o_g:].astype(BF16)
    rest_ref[:, n_gate:] = w[:, o_u:o_g].astype(BF16)


def _realign_w_in(w_in, layer, o_f, o_u, o_g, n_heads, br=64):
    _, d, n = w_in.shape
    assert o_f % LANES == 0 and (n - o_g) % LANES == 0 and (o_g - o_u) % LANES == 0
    br = _pick_block(d, br, 8)
    n_rest = n - o_u
    return pl.pallas_call(
        functools.partial(_realign_kernel, o_f=o_f, o_u=o_u, o_g=o_g, n_heads=n_heads),
        grid=(d // br,),
        in_specs=[pl.BlockSpec((None, br, n), lambda i: (layer, i, 0))],
        out_specs=[pl.BlockSpec((br, o_f), lambda i: (i, 0)), pl.BlockSpec((br, LANES), lambda i: (i, 0)),
                   pl.BlockSpec((br, n_rest), lambda i: (i, 0))],
        out_shape=[jax.ShapeDtypeStruct((d, o_f), BF16), jax.ShapeDtypeStruct((d, LANES), BF16),
                   jax.ShapeDtypeStruct((d, n_rest), BF16)],
        compiler_params=_params("parallel"),
        name="realign_w_in",
    )(w_in)


def _ep_plain(acc):
    return acc


def _ep_headnorm(acc, gain, *, dh, scale):
    outs = []
    for h in range(acc.shape[1] // dh):
        blk = acc[:, h * dh:(h + 1) * dh]
        ms = jnp.mean(blk * blk, axis=-1, keepdims=True)
        outs.append(blk * lax.rsqrt(ms + RMS_EPS) * gain[:, h * dh:(h + 1) * dh] * scale)
    return outs[0] if len(outs) == 1 else jnp.concatenate(outs, axis=1)


def _ep_sigmoid_bias(acc, b):
    return jax.nn.sigmoid(acc + b)


def _ep_glu(acc, b, y):
    return y * jax.nn.sigmoid(acc + b)


def _ep_residual(acc, r):
    return acc + r


def _split3(x):
    hi = x.astype(BF16)
    r1 = x - hi.astype(F32)
    mid = r1.astype(BF16)
    lo = (r1 - mid.astype(F32)).astype(BF16)
    return hi, mid, lo


def _forget_cumsum_kernel(f_ref, b_ref, c_ref, hi_ref, mid_ref, lo_ref, carry_ref, *, bs):
    @pl.when(pl.program_id(0) == 0)
    def _():
        carry_ref[...] = jnp.zeros_like(carry_ref)

    z = f_ref[...] + b_ref[...]
    logf = jnp.minimum(z, 0.0) - jnp.log1p(jnp.exp(-jnp.abs(z)))
    row = lax.broadcasted_iota(I32, (bs, bs), 0)
    col = lax.broadcasted_iota(I32, (bs, bs), 1)
    tri = jnp.where(col <= row, 1.0, 0.0).astype(BF16)
    hi, mid, lo = _split3(logf)
    cs = (jnp.dot(tri, hi, preferred_element_type=F32) + jnp.dot(tri, mid, preferred_element_type=F32)
          + jnp.dot(tri, lo, preferred_element_type=F32)) + carry_ref[...]
    carry_ref[...] = cs[bs - 1:bs, :]
    c2 = cs * LOG2E
    c_ref[...] = c2
    hi_ref[...], mid_ref[...], lo_ref[...] = _split3(c2)


def _forget_cumsum(f_logit, b_forget_pad, bs=256):
    s, w = f_logit.shape
    bs = min(bs, s)
    blk = lambda: pl.BlockSpec((bs, w), lambda i: (i, 0))
    return pl.pallas_call(
        functools.partial(_forget_cumsum_kernel, bs=bs),
        grid=(s // bs,),
        in_specs=[blk(), pl.BlockSpec((1, w), lambda i: (0, 0))],
        out_specs=[blk(), blk(), blk(), blk()],
        out_shape=[jax.ShapeDtypeStruct((s, w), F32)] + [jax.ShapeDtypeStruct((s, w), BF16)] * 3,
        scratch_shapes=[pltpu.VMEM((1, w), F32)],
        compiler_params=_params("arbitrary"),
        name="forget_cumsum",
    )(f_logit, b_forget_pad)


def _fox_kernel(jstart_ref, qt_ref, qb_ref, k_ref, kb_ref, vt_ref, o_ref, m_ref, l_ref, acc_ref, *, nb, blk):
    h, i = pl.program_id(0), pl.program_id(1)
    qt = jnp.concatenate([qt_ref[...], qb_ref[...]], axis=0)

    def rows(j):
        return pl.ds(pl.multiple_of(j * blk, blk), blk)

    def scores(j):
        kj = jnp.concatenate([k_ref[rows(j), :], kb_ref[rows(j), :]], axis=1)
        return jnp.dot(kj, qt, preferred_element_type=F32)

    st = scores(i)
    kpos = lax.broadcasted_iota(I32, st.shape, 0)
    qpos = lax.broadcasted_iota(I32, st.shape, 1)
    st = jnp.where(kpos <= qpos, st, -jnp.inf)
    m0 = jnp.max(st, axis=0, keepdims=True)
    p = jnp.exp2(st - m0)
    m_ref[...] = m0
    l_ref[...] = jnp.sum(p, axis=0, keepdims=True)
    acc_ref[...] = jnp.dot(vt_ref[:, rows(i)], p.astype(BF16), preferred_element_type=F32)

    def body(t, carry):
        j = i - 1 - t
        st = scores(j)
        m_old = m_ref[...]
        m_new = jnp.maximum(m_old, jnp.max(st, axis=0, keepdims=True))
        p = jnp.exp2(st - m_new)
        corr = jnp.exp2(m_old - m_new)
        l_ref[...] = l_ref[...] * corr + jnp.sum(p, axis=0, keepdims=True)
        acc_ref[...] = acc_ref[...] * corr + jnp.dot(vt_ref[:, rows(j)], p.astype(BF16), preferred_element_type=F32)
        m_ref[...] = m_new
        return carry

    lax.fori_loop(0, i - jstart_ref[h * nb + i], body, 0)
    o_ref[...] = (acc_ref[...] * (1.0 / l_ref[...])).T.astype(o_ref.dtype)


def _fox_attention(qt, k, vt, c2, c_pieces, logit_bound, n_heads, dh, blk=512):
    s = k.shape[0]
    blk = min(blk, s)
    nb = s // blk
    pieces = jnp.stack([jnp.transpose(p[:, :n_heads]) for p in c_pieces], axis=1)
    ones = jnp.ones_like(pieces)
    qb = jnp.pad(jnp.concatenate([pieces, ones], axis=1), ((0, 0), (0, dh - 6), (0, 0)))
    kb = jnp.pad(jnp.transpose(jnp.concatenate([ones, -pieces], axis=1), (0, 2, 1)),
                 ((0, 0), (0, 0), (0, dh - 6)))

    ct = jnp.transpose(c2[:, :n_heads]).reshape(n_heads, nb, blk)
    max_bias = ct[:, :, None, 0] - ct[:, None, :, blk - 1]
    ii = jnp.arange(nb)
    needed = jnp.logical_and(max_bias >= -(2.0 * logit_bound + EXP2_UNDERFLOW), ii[None, None, :] <= ii[None, :, None])
    jstart = (ii[None, :] + 1 - jnp.sum(needed, axis=-1)).astype(I32).reshape(n_heads * nb)

    return pl.pallas_call(
        functools.partial(_fox_kernel, nb=nb, blk=blk),
        grid_spec=pltpu.PrefetchScalarGridSpec(
            num_scalar_prefetch=1,
            grid=(n_heads, nb),
            in_specs=[
                pl.BlockSpec((dh, blk), lambda h, i, js: (h, i)),
                pl.BlockSpec((None, dh, blk), lambda h, i, js: (h, 0, i)),
                pl.BlockSpec((s, dh), lambda h, i, js: (0, h)),
                pl.BlockSpec((None, s, dh), lambda h, i, js: (h, 0, 0)),
                pl.BlockSpec((dh, s), lambda h, i, js: (h, 0)),
            ],
            out_specs=pl.BlockSpec((blk, dh), lambda h, i, js: (i, h)),
            scratch_shapes=[pltpu.VMEM((1, blk), F32), pltpu.VMEM((1, blk), F32), pltpu.VMEM((dh, blk), F32)],
        ),
        out_shape=jax.ShapeDtypeStruct((s, n_heads * dh), BF16),
        compiler_params=_params("parallel", "arbitrary"),
        name="fox_attention",
    )(jstart, qt, qb, k, kb, vt)


def _s5_params_kernel(lre_ref, lim_ref, ls_ref, bre_ref, bim_ref, ctre_ref, ctim_ref, cre_ref, cim_ref,
                      zre_ref, zim_ref, wre_ref, wim_ref, m_ref, alre_ref, alim_ref, *, chunk, n_ch):
    lre, lim = lre_ref[...], lim_ref[...]
    step = jnp.exp(ls_ref[...])
    ar, ai = lre * step, lim * step
    lc = chunk * n_ch
    tau = (lax.broadcasted_iota(I32, (1, 1, lc), 2) // n_ch).astype(F32)

    def cexp(t):
        mag = jnp.exp(ar * t)
        return mag * jnp.cos(ai * t), mag * jnp.sin(ai * t)

    e1r, e1i = cexp(1.0)
    nr, ni = e1r - 1.0, e1i
    den = lre * lre + lim * lim
    cr, ci = (nr * lre + ni * lim) / den, (ni * lre - nr * lim) / den
    bre, bim = bre_ref[...], bim_ref[...]
    bbr, bbi = cr * bre - ci * bim, cr * bim + ci * bre
    er, ei = cexp(tau)
    zr, zi = er * bbr - ei * bbi, er * bbi + ei * bbr
    zre_ref[...] = zr
    zim_ref[...] = zi
    e1r_t, e1i_t = cexp(tau + 1.0)
    ctre, ctim = ctre_ref[...], ctim_ref[...]
    wre_ref[...] = ctre * e1r_t - ctim * e1i_t
    wim_ref[...] = ctre * e1i_t + ctim * e1r_t
    dn = (((2,), (1,)), ((0,), (0,)))
    m_ref[...] = (lax.dot_general(cre_ref[...], zr, dn, precision=HIGHEST, preferred_element_type=F32)
                  - lax.dot_general(cim_ref[...], zi, dn, precision=HIGHEST, preferred_element_type=F32))
    alr, ali = cexp(float(chunk))
    alre_ref[...] = alr
    alim_ref[...] = ali


def _s5_operators(lam_re, lam_im, log_step, b_re, b_im, c_re, c_im, chunk):
    g, n = lam_re.shape
    n_ch = b_re.shape[-1]
    gs = LANES // n_ch
    assert g % gs == 0
    nsl, lc = g // gs, chunk * n_ch
    tile = lambda a: jnp.tile(a, (1, 1, chunk))
    ct_re, ct_im = jnp.transpose(c_re, (0, 2, 1)), jnp.transpose(c_im, (0, 2, 1))
    spec3 = lambda a, b: pl.BlockSpec((gs, a, b), lambda i: (i, 0, 0))
    zre, zim, wre, wim, mk, alre, alim = pl.pallas_call(
        functools.partial(_s5_params_kernel, chunk=chunk, n_ch=n_ch),
        grid=(nsl,),
        in_specs=[spec3(n, 1), spec3(n, 1), spec3(1, 1), spec3(n, lc), spec3(n, lc), spec3(n, lc), spec3(n, lc),
                  spec3(n_ch, n), spec3(n_ch, n)],
        out_specs=[spec3(n, lc), spec3(n, lc), spec3(n, lc), spec3(n, lc), spec3(n_ch, lc), spec3(n, 1), spec3(n, 1)],
        out_shape=[jax.ShapeDtypeStruct((g, n, lc), F32)] * 4 + [jax.ShapeDtypeStruct((g, n_ch, lc), F32)]
        + [jax.ShapeDtypeStruct((g, n, 1), F32)] * 2,
        compiler_params=_params("parallel"),
        name="s5_params",
    )(lam_re.reshape(g, n, 1), lam_im.reshape(g, n, 1), log_step.reshape(g, 1, 1), tile(b_re), tile(b_im),
      tile(ct_re), tile(ct_im), c_re, c_im)

    eye = jnp.eye(gs, dtype=F32)
    m5 = mk.reshape(nsl, gs, n_ch, chunk, n_ch)
    kk = jnp.transpose(m5, (0, 3, 1, 4, 2))
    kk = (kk[:, :, :, :, None, :] * eye[None, None, :, None, :, None]).reshape(nsl, chunk, LANES, LANES)
    d = jnp.arange(chunk)[None, :] - jnp.arange(chunk)[:, None]
    tm = jnp.where((d >= 0)[None, :, :, None, None], kk[:, jnp.clip(d, 0, chunk - 1)], 0.0)
    t_op = jnp.transpose(tm, (0, 1, 3, 2, 4)).reshape(nsl, chunk * LANES, chunk * LANES).astype(BF16)

    def inject(z):
        z5 = z.reshape(nsl, gs, n, chunk, n_ch)[:, :, :, ::-1, :]
        pd = jnp.transpose(z5, (0, 3, 1, 4, 2))
        return (pd[:, :, :, :, None, :] * eye[None, None, :, None, :, None]).reshape(nsl, chunk * LANES, gs * n)

    def readout(w):
        w5 = w.reshape(nsl, gs, n, chunk, n_ch)
        return (w5[:, :, :, :, None, :] * eye[None, :, None, None, :, None]).reshape(nsl, gs * n, chunk * LANES)

    p_op = jnp.concatenate([inject(zre), inject(zim)], axis=-1).astype(BF16)
    q_op = jnp.concatenate([readout(wre), -readout(wim)], axis=1).astype(BF16)
    a_l = jnp.concatenate([alre.reshape(nsl, 1, gs * n), alim.reshape(nsl, 1, gs * n)], axis=-1)
    return t_op, p_op, q_op, a_l


def _gelu_tanh(x):
    return 0.5 * x * (1.0 + jnp.tanh(math.sqrt(2.0 / math.pi) * (x + 0.044715 * (x * x * x))))


def _s5_conv_kernel(u_ref, t_ref, p_ref, q_ref, a_ref, d_ref, o_ref, hst_ref, s_ref, hp_ref, *, chunk, bk):
    hn = a_ref.shape[1] // 2

    @pl.when(pl.program_id(1) == 0)
    def _():
        hst_ref[...] = jnp.zeros_like(hst_ref)

    x = jnp.concatenate([u_ref[pl.ds(s, bk, stride=chunk), :] for s in range(chunk)], axis=1).astype(BF16)
    s_ref[...] = jnp.dot(x, p_ref[...], preferred_element_type=F32)
    ar, ai = a_ref[:, :hn], a_ref[:, hn:]

    def body(k, h):
        hr, hi = h
        hp_ref[pl.ds(k, 1), :hn] = hr
        hp_ref[pl.ds(k, 1), hn:] = hi
        row = s_ref[pl.ds(k, 1), :]
        return ar * hr - ai * hi + row[:, :hn], ar * hi + ai * hr + row[:, hn:]

    hr, hi = lax.fori_loop(0, bk, body, (hst_ref[:, :hn], hst_ref[:, hn:]))
    hst_ref[:, :hn] = hr
    hst_ref[:, hn:] = hi

    y = (jnp.dot(x, t_ref[...], preferred_element_type=F32)
         + jnp.dot(hp_ref[...].astype(BF16), q_ref[...], preferred_element_type=F32))
    d = d_ref[...]
    for t in range(chunk):
        yt = y[:, t * LANES:(t + 1) * LANES] + d * u_ref[pl.ds(t, bk, stride=chunk), :]
        o_ref[pl.ds(t, bk, stride=chunk), :] = _gelu_tanh(yt)


def _s5_conv(u, t_op, p_op, q_op, a_l, d_skip, chunk, bk=512):
    s, p = u.shape
    nsl = p // LANES
    bk = min(bk, s // chunk)
    bm = bk * chunk
    hn2 = a_l.shape[-1]
    return pl.pallas_call(
        functools.partial(_s5_conv_kernel, chunk=chunk, bk=bk),
        grid=(nsl, s // bm),
        in_specs=[
            pl.BlockSpec((bm, LANES), lambda sl, i: (i, sl)),
            pl.BlockSpec((None, chunk * LANES, chunk * LANES), lambda sl, i: (sl, 0, 0)),
            pl.BlockSpec((None, chunk * LANES, hn2), lambda sl, i: (sl, 0, 0)),
            pl.BlockSpec((None, hn2, chunk * LANES), lambda sl, i: (sl, 0, 0)),
            pl.BlockSpec((None, 1, hn2), lambda sl, i: (sl, 0, 0)),
            pl.BlockSpec((1, LANES), lambda sl, i: (0, sl)),
        ],
        out_specs=pl.BlockSpec((bm, LANES), lambda sl, i: (i, sl)),
        out_shape=jax.ShapeDtypeStruct((s, p), F32),
        scratch_shapes=[pltpu.VMEM((1, hn2), F32), pltpu.VMEM((bk, hn2), F32), pltpu.VMEM((bk, hn2), F32)],
        compiler_params=_params("parallel", "arbitrary"),
        name="s5_conv",
    )(u, t_op, p_op, q_op, a_l, d_skip.reshape(1, p))


def _mem_attn_kernel(q_ref, k_ref, v_ref, o_ref, *, n_heads, dm):
    for h in range(n_heads):
        sl = slice(h * dm, (h + 1) * dm)
        s = lax.dot_general(q_ref[:, sl], k_ref[:, sl], (((1,), (1,)), ((), ())), preferred_element_type=F32)
        p = jnp.exp(s - jnp.max(s, axis=-1, keepdims=True))
        o = jnp.dot(p.astype(BF16), v_ref[:, sl], preferred_element_type=F32) / jnp.sum(p, axis=-1, keepdims=True)
        o_ref[:, sl] = o.astype(o_ref.dtype)


def _mem_attention(q, k, v, n_heads, dm, bm=512):
    s, w = q.shape
    m = k.shape[0]
    bm = min(bm, s)
    return pl.pallas_call(
        functools.partial(_mem_attn_kernel, n_heads=n_heads, dm=dm),
        grid=(s // bm,),
        in_specs=[pl.BlockSpec((bm, w), lambda i: (i, 0)), pl.BlockSpec((m, w), lambda i: (0, 0)),
                  pl.BlockSpec((m, w), lambda i: (0, 0))],
        out_specs=pl.BlockSpec((bm, w), lambda i: (i, 0)),
        out_shape=jax.ShapeDtypeStruct((s, w), BF16),
        compiler_params=_params("parallel"),
        name="mem_attention",
    )(q, k, v)


def _merge_kernel(ya_ref, ys_ref, ym_ref, wa_ref, ws_ref, wm_ref, ga_ref, gs_ref, gm_ref, o_ref):
    a = jnp.dot(ya_ref[...], wa_ref[...], preferred_element_type=F32)
    s = jnp.dot(ys_ref[...], ws_ref[...], preferred_element_type=F32)
    m = jnp.dot(ym_ref[...], wm_ref[...], preferred_element_type=F32)
    o_ref[...] = (ga_ref[...].astype(F32) * a + gs_ref[...].astype(F32) * s
                  + gm_ref[...].astype(F32) * m).astype(o_ref.dtype)


def _merge(ya, ys, ym, wa, ws, wm, gates, bm=1024, bn=512):
    s, d = ya.shape[0], wa.shape[1]
    bm, bn = min(bm, s), min(bn, d)
    nb = d // bn
    row = lambda a: pl.BlockSpec((bm, a.shape[1]), lambda i, j: (i, 0))
    col = lambda a: pl.BlockSpec((a.shape[0], bn), lambda i, j: (0, j))
    gate = lambda b: pl.BlockSpec((bm, bn), lambda i, j, b=b: (i, j + b * nb))
    return pl.pallas_call(
        _merge_kernel,
        grid=(s // bm, nb),
        in_specs=[row(ya), row(ys), row(ym), col(wa), col(ws), col(wm), gate(0), gate(1), gate(2)],
        out_specs=pl.BlockSpec((bm, bn), lambda i, j: (i, j)),
        out_shape=jax.ShapeDtypeStruct((s, d), BF16),
        compiler_params=_params("parallel", "parallel"),
        name="merge",
    )(ya, ys, ym, wa, ws, wm, gates, gates, gates)


def _router_kernel(x_ref, g_ref, wr_ref, br_ref, xp_ref, idx_ref, gate_ref, rank_ref, cnt_ref, carry_ref, *, bm):
    @pl.when(pl.program_id(0) == 0)
    def _():
        carry_ref[...] = jnp.zeros_like(carry_ref)

    x = x_ref[...]
    ms = jnp.mean(x * x, axis=-1, keepdims=True)
    xn = x * lax.rsqrt(ms + RMS_EPS) * g_ref[...]
    d2 = xn.shape[1] // 2
    xp_ref[...] = _pack_bf16_pair(xn[:, :d2], xn[:, d2:])

    n_exp = wr_ref.shape[1]
    x_hi, w_hi = xn.astype(BF16), wr_ref[...].astype(BF16)
    x_lo = (xn - x_hi.astype(F32)).astype(BF16)
    w_lo = (wr_ref[...] - w_hi.astype(F32)).astype(BF16)
    logits = (jnp.dot(x_hi, w_hi, preferred_element_type=F32) + jnp.dot(x_lo, w_hi, preferred_element_type=F32)
              + jnp.dot(x_hi, w_lo, preferred_element_type=F32)) + br_ref[...]
    lane = lax.broadcasted_iota(I32, logits.shape, 1)
    kcol = lax.broadcasted_iota(I32, (bm, TOP_K), 1)
    work = logits
    member = jnp.zeros_like(logits)
    idxs, vals = [], []
    for _ in range(TOP_K):
        mx = jnp.max(work, axis=-1, keepdims=True)
        ik = jnp.min(jnp.where(work == mx, lane, n_exp), axis=-1, keepdims=True)
        sel = lane == ik
        member = jnp.where(sel, 1.0, member)
        work = jnp.where(sel, -jnp.inf, work)
        idxs.append(ik)
        vals.append(mx)
    es = [jnp.exp(v - vals[0]) for v in vals]
    den = es[0] + es[1] + es[2] + es[3]

    r = lax.broadcasted_iota(I32, (bm, bm), 0)
    c = lax.broadcasted_iota(I32, (bm, bm), 1)
    tri = jnp.where(c < r, 1.0, 0.0).astype(BF16)
    prefix = jnp.dot(tri, member.astype(BF16), preferred_element_type=F32) + carry_ref[...]
    total = carry_ref[...] + jnp.sum(member, axis=0, keepdims=True)
    carry_ref[...] = total
    cnt_ref[...] = total

    idx_o = jnp.zeros((bm, TOP_K), I32)
    gate_o = jnp.zeros((bm, TOP_K), F32)
    rank_o = jnp.zeros((bm, TOP_K), F32)
    for k in range(TOP_K):
        rk = jnp.sum(jnp.where(lane == idxs[k], prefix, 0.0), axis=-1, keepdims=True)
        idx_o = jnp.where(kcol == k, idxs[k], idx_o)
        gate_o = jnp.where(kcol == k, es[k] / den, gate_o)
        rank_o = jnp.where(kcol == k, rk, rank_o)
    idx_ref[...] = idx_o
    gate_ref[...] = gate_o
    rank_ref[...] = rank_o.astype(I32)


def _router(x1, g, w_router, b_router, bm=256):
    t, d = x1.shape
    n_exp = w_router.shape[1]
    bm = min(bm, t)
    small = lambda: pl.BlockSpec((bm, TOP_K), lambda i: (i, 0))
    return pl.pallas_call(
        functools.partial(_router_kernel, bm=bm),
        grid=(t // bm,),
        in_specs=[pl.BlockSpec((bm, d), lambda i: (i, 0)), pl.BlockSpec((1, d), lambda i: (0, 0)),
                  pl.BlockSpec((d, n_exp), lambda i: (0, 0)), pl.BlockSpec((1, n_exp), lambda i: (0, 0))],
        out_specs=[pl.BlockSpec((bm, d // 2), lambda i: (i, 0)), small(), small(), small(),
                   pl.BlockSpec((1, n_exp), lambda i: (0, 0))],
        out_shape=[jax.ShapeDtypeStruct((t, d // 2), U32), jax.ShapeDtypeStruct((t, TOP_K), I32),
                   jax.ShapeDtypeStruct((t, TOP_K), F32), jax.ShapeDtypeStruct((t, TOP_K), I32),
                   jax.ShapeDtypeStruct((1, n_exp), F32)],
        scratch_shapes=[pltpu.VMEM((1, n_exp), F32)],
        compiler_params=_params("arbitrary"),
        name="router",
    )(x1, g.reshape(1, d), w_router, b_router.reshape(1, n_exp))


def _dispatch_kernel(fill_ref, dest_ref, xp_ref, xs_ref, zero_ref, sem_z, sem_r, *, bm, bm_e, n_tiles):
    @pl.when(pl.program_id(0) == 0)
    def _():
        zero_ref[...] = jnp.zeros_like(zero_ref)

        def fill(t, wait):
            @pl.when(fill_ref[t] > 0)
            def _():
                cp = pltpu.make_async_copy(zero_ref, xs_ref.at[pl.ds(t * bm_e, bm_e), :], sem_z)
                cp.wait() if wait else cp.start()

        lax.fori_loop(0, n_tiles, lambda t, c: (fill(t, False), c)[1], 0)
        lax.fori_loop(0, n_tiles, lambda t, c: (fill(t, True), c)[1], 0)

    def row_copy(r, k):
        return pltpu.make_async_copy(xp_ref.at[pl.ds(r, 1), :], xs_ref.at[pl.ds(dest_ref[r * TOP_K + k], 1), :], sem_r)

    def issue(r, c):
        for k in range(TOP_K):
            row_copy(r, k).start()
        return c

    def drain(r, c):
        for k in range(TOP_K):
            row_copy(r, k).wait()
        return c

    lax.fori_loop(0, bm, issue, 0)
    lax.fori_loop(0, bm, drain, 0)


def _dispatch(xp, dest_flat, tile_fill, bm_e, bm=256):
    t, d2 = xp.shape
    bm = min(bm, t)
    n_tiles = tile_fill.shape[0]
    n_rows = n_tiles * bm_e
    return pl.pallas_call(
        functools.partial(_dispatch_kernel, bm=bm, bm_e=bm_e, n_tiles=n_tiles),
        grid_spec=pltpu.PrefetchScalarGridSpec(
            num_scalar_prefetch=1,
            grid=(t // bm,),
            in_specs=[pl.BlockSpec((bm * TOP_K,), lambda i, lt: (i,), memory_space=pltpu.SMEM),
                      pl.BlockSpec((bm, d2), lambda i, lt: (i, 0))],
            out_specs=pl.BlockSpec(memory_space=pl.ANY),
            scratch_shapes=[pltpu.VMEM((bm_e, d2), U32), pltpu.SemaphoreType.DMA(()), pltpu.SemaphoreType.DMA(())],
        ),
        out_shape=jax.ShapeDtypeStruct((n_rows, d2), U32),
        compiler_params=_params("arbitrary"),
        name="moe_dispatch",
    )(tile_fill, dest_flat, xp)


FIRST_VISIT, TILE_USED = 1, 2


def _work_list(tile_start, tile_count, n_used, n_tiles, n_chunks):
    n_exp = tile_start.shape[0]
    w = jnp.arange(n_tiles * n_chunks, dtype=I32)
    e = jnp.minimum(jnp.sum((n_chunks * (tile_start + tile_count))[None, :] <= w[:, None], axis=1), n_exp - 1).astype(I32)
    cnt = jnp.maximum(tile_count[e], 1)
    r = w - n_chunks * tile_start[e]
    used = w < n_chunks * n_used
    n_unused = jnp.maximum(n_tiles - n_used, 1)
    r2 = w - n_chunks * n_used
    tile = jnp.where(used, tile_start[e] + r % cnt, n_used + r2 % n_unused)
    chunk = jnp.where(used, r // cnt, r2 // n_unused)
    last = n_chunks * n_used - 1
    in_tile = jnp.where(used, tile, tile[last])
    w_exp = jnp.where(used, e, e[last])
    w_chunk = jnp.where(used, chunk, chunk[last])
    flags = jnp.where(used, TILE_USED + jnp.where(r % cnt == 0, FIRST_VISIT, 0), 0)
    return [a.astype(I32) for a in (tile, chunk, in_tile, w_exp, w_chunk, flags)]


def _moe_up_kernel(tile_ref, chunk_ref, xin_ref, we_ref, wc_ref, flag_ref, xs_ref, wg_ref, wl_ref, bg_ref, bl_ref,
                   o_ref, wb_ref):
    flag = flag_ref[pl.program_id(0)]
    tf = wg_ref.shape[1]

    @pl.when((flag & FIRST_VISIT) != 0)
    def _():
        wb_ref[:, :tf] = wg_ref[...].astype(BF16)
        wb_ref[:, tf:] = wl_ref[...].astype(BF16)

    @pl.when((flag & TILE_USED) != 0)
    def _():
        lo, hi = _unpack_bf16_pair(xs_ref[...])
        xlo, xhi = lo.astype(BF16), hi.astype(BF16)
        d2 = xlo.shape[1]
        hcat = (jnp.dot(xlo, wb_ref[:d2, :], preferred_element_type=F32)
                + jnp.dot(xhi, wb_ref[d2:, :], preferred_element_type=F32))
        glu = jnp.minimum(hcat[:, :tf] + bg_ref[...], SWIGLU_LIMIT)
        lin = jnp.clip(hcat[:, tf:] + bl_ref[...], -SWIGLU_LIMIT, SWIGLU_LIMIT)
        o_ref[...] = (glu * jax.nn.sigmoid(SWIGLU_ALPHA * glu) * (lin + 1.0)).astype(o_ref.dtype)

    @pl.when((flag & TILE_USED) == 0)
    def _():
        o_ref[...] = jnp.zeros_like(o_ref)


def _moe_up(xs, work, w1, b1, bm_e, tf):
    n_rows, d2 = xs.shape
    n_exp, d, f2 = w1.shape
    ff = f2 // 2
    nf = ff // tf
    wspec = lambda off: pl.BlockSpec((None, d, tf), lambda w, t, c, xi, we, wc, fl: (we[w], 0, wc[w] + off))
    bspec = lambda off: pl.BlockSpec((None, 1, tf), lambda w, t, c, xi, we, wc, fl: (we[w], 0, wc[w] + off))
    return pl.pallas_call(
        _moe_up_kernel,
        grid_spec=pltpu.PrefetchScalarGridSpec(
            num_scalar_prefetch=6,
            grid=(work[0].shape[0],),
            in_specs=[pl.BlockSpec((bm_e, d2), lambda w, t, c, xi, we, wc, fl: (xi[w], 0)),
                      wspec(0), wspec(nf), bspec(0), bspec(nf)],
            out_specs=pl.BlockSpec((bm_e, tf), lambda w, t, c, xi, we, wc, fl: (t[w], c[w])),
            scratch_shapes=[pltpu.VMEM((d, 2 * tf), BF16)],
        ),
        out_shape=jax.ShapeDtypeStruct((n_rows, ff), BF16),
        compiler_params=_params("arbitrary"),
        name="moe_up",
    )(*work, xs, w1, w1, b1.reshape(n_exp, 1, f2), b1.reshape(n_exp, 1, f2))


def _moe_down_kernel(tile_ref, chunk_ref, hin_ref, we_ref, wc_ref, flag_ref, h_ref, w2_ref, b2_ref, o_ref, w2b_ref):
    flag = flag_ref[pl.program_id(0)]

    @pl.when((flag & FIRST_VISIT) != 0)
    def _():
        w2b_ref[...] = w2_ref[...].astype(BF16)

    @pl.when((flag & TILE_USED) != 0)
    def _():
        y = jnp.dot(h_ref[...], w2b_ref[...], preferred_element_type=F32) + b2_ref[...]
        half = y.shape[1] // 2
        o_ref[...] = _pack_bf16_pair(y[:, :half], y[:, half:])

    @pl.when((flag & TILE_USED) == 0)
    def _():
        o_ref[...] = jnp.zeros_like(o_ref)


def _moe_down(hid, work, w2, b2, bm_e, tn):
    n_rows, ff = hid.shape
    n_exp, _, d = w2.shape
    return pl.pallas_call(
        _moe_down_kernel,
        grid_spec=pltpu.PrefetchScalarGridSpec(
            num_scalar_prefetch=6,
            grid=(work[0].shape[0],),
            in_specs=[pl.BlockSpec((bm_e, ff), lambda w, t, c, hi, we, wc, fl: (hi[w], 0)),
                      pl.BlockSpec((None, ff, tn), lambda w, t, c, hi, we, wc, fl: (we[w], 0, wc[w])),
                      pl.BlockSpec((None, 1, tn), lambda w, t, c, hi, we, wc, fl: (we[w], 0, wc[w]))],
            out_specs=pl.BlockSpec((bm_e, tn // 2), lambda w, t, c, hi, we, wc, fl: (t[w], c[w])),
            scratch_shapes=[pltpu.VMEM((ff, tn), BF16)],
        ),
        out_shape=jax.ShapeDtypeStruct((n_rows, d // 2), U32),
        compiler_params=_params("arbitrary"),
        name="moe_down",
    )(*work, hid, w2, b2.reshape(n_exp, 1, d))


def _combine_kernel(dest_ref, dest_next_ref, x_ref, gate_ref, ys_ref, o_ref, buf_ref, sem, *, bm, tn):
    i, n_steps = pl.program_id(0), pl.num_programs(0)
    slot = i % 2

    def row_copy(d_ref, s, r, k):
        return pltpu.make_async_copy(ys_ref.at[pl.ds(d_ref[r * TOP_K + k], 1), :],
                                     buf_ref.at[s, k, pl.ds(r, 1), :], sem.at[s])

    def gather(d_ref, s, wait):
        def body(r, c):
            for k in range(TOP_K):
                cp = row_copy(d_ref, s, r, k)
                cp.wait() if wait else cp.start()
            return c
        lax.fori_loop(0, bm, body, 0)

    @pl.when(i == 0)
    def _():
        gather(dest_ref, slot, False)

    @pl.when(i + 1 < n_steps)
    def _():
        gather(dest_next_ref, 1 - slot, False)

    gather(dest_ref, slot, True)
    g = gate_ref[...]
    half = tn // 2
    for n in range(x_ref.shape[1] // tn):
        acc_lo = x_ref[:, n * tn:n * tn + half]
        acc_hi = x_ref[:, n * tn + half:(n + 1) * tn]
        for k in range(TOP_K):
            lo, hi = _unpack_bf16_pair(buf_ref[slot, k, :, n * half:(n + 1) * half])
            acc_lo = acc_lo + g[:, k:k + 1] * lo
            acc_hi = acc_hi + g[:, k:k + 1] * hi
        o_ref[:, n * tn:n * tn + half] = acc_lo
        o_ref[:, n * tn + half:(n + 1) * tn] = acc_hi


def _combine(x1, gate, dest_flat, ys, tn, bm=128):
    t, d = x1.shape
    bm = min(bm, t)
    n_steps = t // bm
    return pl.pallas_call(
        functools.partial(_combine_kernel, bm=bm, tn=tn),
        grid=(n_steps,),
        in_specs=[pl.BlockSpec((bm * TOP_K,), lambda i: (i,), memory_space=pltpu.SMEM),
                  pl.BlockSpec((bm * TOP_K,), lambda i: (jnp.minimum(i + 1, n_steps - 1),), memory_space=pltpu.SMEM),
                  pl.BlockSpec((bm, d), lambda i: (i, 0)),
                  pl.BlockSpec((bm, TOP_K), lambda i: (i, 0)),
                  pl.BlockSpec(memory_space=pl.ANY)],
        out_specs=pl.BlockSpec((bm, d), lambda i: (i, 0)),
        out_shape=jax.ShapeDtypeStruct((t, d), F32),
        scratch_shapes=[pltpu.VMEM((2, TOP_K, bm, d // 2), U32), pltpu.SemaphoreType.DMA((2,))],
        compiler_params=_params("arbitrary"),
        name="moe_combine",
    )(dest_flat, dest_flat, x1, gate, ys)


def _mixer(x2, mem2, layer, norm_mix, norm_mem, w_in, b_gate, b_forget, q_norm_attn, k_norm_attn, q_norm_mem, k_norm_mem,
           w_mem_kv, lam_re, lam_im, log_step, ssm_b_re, ssm_b_im, ssm_c_re, ssm_c_im, ssm_d, w_glu, b_glu,
           w_up_attn, w_up_ssm, w_up_mem, w_out):
    s, d = x2.shape
    n_heads, dh = b_forget.shape[0], q_norm_attn.shape[0]
    aw = n_heads * dh
    pw = ssm_d.shape[0]
    dm = q_norm_mem.shape[0]
    mw = w_up_mem.shape[0]
    mem_heads = mw // dm
    o_k, o_v, o_f, o_u, o_qm, o_g = aw, 2 * aw, 3 * aw, 3 * aw + n_heads, 3 * aw + n_heads + pw, 3 * aw + n_heads + pw + mw
    bf = lambda a: a.astype(BF16)

    h = _rmsnorm(x2, norm_mix, BF16)
    w_qkv, w_f, w_rest = _realign_w_in(w_in, layer, o_f, o_u, o_g, n_heads)
    qt = _matmul(h, w_qkv, [(jnp.tile(q_norm_attn, n_heads).reshape(1, aw), "row", 0)],
                 functools.partial(_ep_headnorm, dh=dh, scale=dh ** -0.5 * LOG2E), BF16, 1024, 512, "in_q",
                 w_cols=(0, aw), transpose_out=True)
    k = _matmul(h, w_qkv, [(jnp.tile(k_norm_attn, n_heads).reshape(1, aw), "row", 0)],
                functools.partial(_ep_headnorm, dh=dh, scale=1.0), BF16, 1024, 512, "in_k", w_cols=(o_k, aw))
    vt = _matmul(h, w_qkv, [], _ep_plain, BF16, 1024, 512, "in_v", w_cols=(o_v, aw), transpose_out=True)
    f_logit = _matmul(h, w_f, [], _ep_plain, F32, 1024, LANES, "in_f")
    gates = _matmul(h, w_rest, [(b_gate.reshape(1, 3 * d), "row", 0)], _ep_sigmoid_bias, BF16, 1024, 1024,
                    "in_gates", w_cols=(0, 3 * d))
    u = _matmul(h, w_rest, [], _ep_plain, F32, 1024, 512, "in_u", w_cols=(3 * d, pw))
    qm = _matmul(h, w_rest, [(jnp.tile(q_norm_mem, mem_heads).reshape(1, mw), "row", 0)],
                 functools.partial(_ep_headnorm, dh=dm, scale=dm ** -0.5), BF16, 1024, 512, "in_qm",
                 w_cols=(3 * d + pw, mw))

    c2, *c_pieces = _forget_cumsum(f_logit, jnp.pad(b_forget, (0, LANES - n_heads)).reshape(1, LANES))
    logit_bound = 1.02 * LOG2E * dh ** 0.5 * jnp.max(jnp.abs(q_norm_attn)) * jnp.max(jnp.abs(k_norm_attn))
    y_attn = _fox_attention(qt, k, vt, c2, c_pieces, logit_bound, n_heads, dh)

    t_op, p_op, q_op, a_l = _s5_operators(lam_re, lam_im, log_step, ssm_b_re, ssm_b_im, ssm_c_re, ssm_c_im, SSM_CHUNK)
    yg = _s5_conv(u, t_op, p_op, q_op, a_l, ssm_d, SSM_CHUNK)
    y_ssm = _matmul(yg, bf(w_glu), [(b_glu.reshape(1, pw), "row", 0), (yg, "tile", 0)], _ep_glu, BF16, 1024, 512, "s5_glu")

    mem_h = _rmsnorm(mem2, norm_mem, BF16)
    k_m = _matmul(mem_h, bf(w_mem_kv[:, :mw]), [(jnp.tile(k_norm_mem, mem_heads).reshape(1, mw), "row", 0)],
                  functools.partial(_ep_headnorm, dh=dm, scale=1.0), BF16, 256, 512, "mem_k")
    v_m = _matmul(mem_h, bf(w_mem_kv[:, mw:]), [], _ep_plain, BF16, 256, 512, "mem_v")
    y_mem = _mem_attention(qm, k_m, v_m, mem_heads, dm)

    merged = _merge(y_attn, y_ssm, y_mem, bf(w_up_attn), bf(w_up_ssm), bf(w_up_mem), gates)
    return _matmul(merged, bf(w_out), [(x2, "tile", 0)], _ep_residual, F32, 1024, 512, "out_proj")


def _moe(x1, norm_ffn, w_router, b_router, exp_w1, exp_b1, exp_w2, exp_b2, bm_e=512):
    t, d = x1.shape
    n_exp = w_router.shape[1]
    xp, idx, gate, rank, counts = _router(x1, norm_ffn, w_router, b_router)

    counts = counts.reshape(n_exp).astype(I32)
    padded = (counts + bm_e - 1) // bm_e * bm_e
    pend = jnp.cumsum(padded)
    pstart = pend - padded
    dest_flat = (pstart[idx] + rank).reshape(t * TOP_K)
    n_tiles = -(-(t * TOP_K) // bm_e) + n_exp
    n_used = (pend[-1] // bm_e).astype(I32)
    tiles = jnp.arange(n_tiles, dtype=I32)
    has_pad = jnp.zeros((n_tiles,), I32).at[pend // bm_e - 1].max((padded > counts).astype(I32))
    tile_fill = jnp.where(tiles < n_used, has_pad, 1)
    tf = _pick_block(exp_w2.shape[1], MOE_UP_CHUNK)
    tn = _pick_block(d, MOE_DOWN_CHUNK)
    work_up = _work_list(pstart // bm_e, padded // bm_e, n_used, n_tiles, exp_w2.shape[1] // tf)
    work_down = _work_list(pstart // bm_e, padded // bm_e, n_used, n_tiles, d // tn)

    xs = _dispatch(xp, dest_flat, tile_fill, bm_e)
    hid = _moe_up(xs, work_up, exp_w1, exp_b1, bm_e, tf)
    ys = _moe_down(hid, work_down, exp_w2, exp_b2, bm_e, tn)
    return _combine(x1, gate, dest_flat, ys, tn)


def kernel(x, mem, norm_mix, norm_ffn, norm_mem, w_in, b_gate, b_forget, q_norm_attn, k_norm_attn, q_norm_mem, k_norm_mem, w_mem_kv, lam_re, lam_im, log_step, ssm_b_re, ssm_b_im, ssm_c_re, ssm_c_im, ssm_d, w_glu, b_glu, w_up_attn, w_up_ssm, w_up_mem, w_out, w_router, b_router, exp_w1, exp_b1, exp_w2, exp_b2):
    b, s, d = x.shape
    assert b == 1, "one sequence per call"
    x2 = x.reshape(s, d)
    for l in range(norm_mix.shape[0]):
        x2 = _mixer(x2, mem.reshape(mem.shape[1], d), l, norm_mix[l], norm_mem[l], w_in, b_gate[l], b_forget[l],
                    q_norm_attn[l], k_norm_attn[l], q_norm_mem[l], k_norm_mem[l], w_mem_kv[l], lam_re[l], lam_im[l],
                    log_step[l], ssm_b_re[l], ssm_b_im[l], ssm_c_re[l], ssm_c_im[l], ssm_d[l], w_glu[l], b_glu[l],
                    w_up_attn[l], w_up_ssm[l], w_up_mem[l], w_out[l])
        x2 = _moe(x2, norm_ffn[l], w_router[l], b_router[l], exp_w1[l], exp_b1[l], exp_w2[l], exp_b2[l])
    return x2.reshape(b, s, d)
```

```python
import functools
import math

import jax
import jax.numpy as jnp
from jax import lax
from jax.experimental import pallas as pl
from jax.experimental.pallas import tpu as pltpu

F32 = jnp.float32
BF16 = jnp.bfloat16
I32 = jnp.int32
U32 = jnp.uint32

RMS_EPS = 1e-6
TOP_K = 4
SWIGLU_ALPHA = 1.702
SWIGLU_LIMIT = 7.0

LANES = 128
VMEM_LIMIT = 56 * 1024 * 1024
SSM_CHUNK = 8
MOE_UP_CHUNK = 384
MOE_DOWN_CHUNK = 2048
HIGHEST = lax.Precision.HIGHEST
LOG2E = math.log2(math.e)
EXP2_UNDERFLOW = 160.0


def _pick_block(n, target, align=LANES):
    if n <= target:
        return n
    b = target // align * align
    while n % b:
        b -= align
    return b


def _pack_bf16_pair(lo, hi):
    a = pltpu.bitcast(lo.astype(BF16).astype(F32), U32) >> 16
    b = pltpu.bitcast(hi.astype(BF16).astype(F32), U32) & jnp.uint32(0xFFFF0000)
    return a | b


def _unpack_bf16_pair(w):
    return pltpu.bitcast(w << 16, F32), pltpu.bitcast(w & jnp.uint32(0xFFFF0000), F32)


def _params(*sem):
    return pltpu.CompilerParams(dimension_semantics=sem, vmem_limit_bytes=VMEM_LIMIT)


def _rmsnorm_kernel(x_ref, g_ref, o_ref):
    x = x_ref[...]
    ms = jnp.mean(x * x, axis=-1, keepdims=True)
    o_ref[...] = (x * lax.rsqrt(ms + RMS_EPS) * g_ref[...]).astype(o_ref.dtype)


def _rmsnorm(x, g, out_dtype, bm=256):
    m, d = x.shape
    bm = min(bm, m)
    return pl.pallas_call(
        _rmsnorm_kernel,
        grid=(m // bm,),
        in_specs=[pl.BlockSpec((bm, d), lambda i: (i, 0)), pl.BlockSpec((1, d), lambda i: (0, 0))],
        out_specs=pl.BlockSpec((bm, d), lambda i: (i, 0)),
        out_shape=jax.ShapeDtypeStruct((m, d), out_dtype),
        compiler_params=_params("parallel"),
        name="rmsnorm",
    )(x, g.reshape(1, d))


def _mm_kernel(*refs, n_extra, epilogue, transpose_out):
    x_ref, w_ref = refs[0], refs[1]
    extras = refs[2:2 + n_extra]
    o_ref = refs[2 + n_extra]
    acc = jnp.dot(x_ref[...].astype(BF16), w_ref[...], preferred_element_type=F32)
    out = epilogue(acc, *[e[...] for e in extras])
    o_ref[...] = (out.T if transpose_out else out).astype(o_ref.dtype)


def _matmul(x, w, extras, epilogue, out_dtype, bm, bn, name, w_cols=None, transpose_out=False):
    m, k = x.shape
    c0, n = w_cols if w_cols is not None else (0, w.shape[1])
    bm, bn = _pick_block(m, bm, 8), _pick_block(n, bn)
    while c0 % bn or n % bn:
        bn -= LANES
    wj = c0 // bn
    in_specs = [pl.BlockSpec((bm, k), lambda i, j: (i, 0)), pl.BlockSpec((k, bn), lambda i, j: (0, j + wj))]
    args = [x, w]
    for arr, kind, off in extras:
        if kind == "row":
            in_specs.append(pl.BlockSpec((1, bn), lambda i, j, off=off: (0, j + off)))
        else:
            in_specs.append(pl.BlockSpec((bm, bn), lambda i, j, off=off: (i, j + off)))
        args.append(arr)
    if transpose_out:
        out_spec, out_shape = pl.BlockSpec((bn, bm), lambda i, j: (j, i)), (n, m)
    else:
        out_spec, out_shape = pl.BlockSpec((bm, bn), lambda i, j: (i, j)), (m, n)
    return pl.pallas_call(
        functools.partial(_mm_kernel, n_extra=len(extras), epilogue=epilogue, transpose_out=transpose_out),
        grid=(m // bm, n // bn),
        in_specs=in_specs,
        out_specs=out_spec,
        out_shape=jax.ShapeDtypeStruct(out_shape, out_dtype),
        compiler_params=_params("parallel", "parallel"),
        name=name,
    )(*args)


def _realign_kernel(w_ref, main_ref, f_ref, rest_ref, *, o_f, o_u, o_g, n_heads):
    w = w_ref[...]
    main_ref[...] = w[:, :o_f].astype(BF16)
    fcols = w[:, o_f:o_f + LANES]
    lane = lax.broadcasted_iota(I32, fcols.shape, 1)
    f_ref[...] = jnp.where(lane < n_heads, fcols, 0.0).astype(BF16)
    n_gate = w.shape[1] - o_g
    rest_ref[:, :n_gate] = w[:, o_g:].astype(BF16)
    rest_ref[:, n_gate:] = w[:, o_u:o_g].astype(BF16)


def _realign_w_in(w_in, layer, o_f, o_u, o_g, n_heads, br=64):
    _, d, n = w_in.shape
    assert o_f % LANES == 0 and (n - o_g) % LANES == 0 and (o_g - o_u) % LANES == 0
    br = _pick_block(d, br, 8)
    n_rest = n - o_u
    return pl.pallas_call(
        functools.partial(_realign_kernel, o_f=o_f, o_u=o_u, o_g=o_g, n_heads=n_heads),
        grid=(d // br,),
        in_specs=[pl.BlockSpec((None, br, n), lambda i: (layer, i, 0))],
        out_specs=[pl.BlockSpec((br, o_f), lambda i: (i, 0)), pl.BlockSpec((br, LANES), lambda i: (i, 0)),
                   pl.BlockSpec((br, n_rest), lambda i: (i, 0))],
        out_shape=[jax.ShapeDtypeStruct((d, o_f), BF16), jax.ShapeDtypeStruct((d, LANES), BF16),
                   jax.ShapeDtypeStruct((d, n_rest), BF16)],
        compiler_params=_params("parallel"),
        name="realign_w_in",
    )(w_in)


def _ep_plain(acc):
    return acc


def _ep_headnorm(acc, gain, *, dh, scale):
    outs = []
    for h in range(acc.shape[1] // dh):
        blk = acc[:, h * dh:(h + 1) * dh]
        ms = jnp.mean(blk * blk, axis=-1, keepdims=True)
        outs.append(blk * lax.rsqrt(ms + RMS_EPS) * gain[:, h * dh:(h + 1) * dh] * scale)
    return outs[0] if len(outs) == 1 else jnp.concatenate(outs, axis=1)


def _ep_sigmoid_bias(acc, b):
    return jax.nn.sigmoid(acc + b)


def _ep_glu(acc, b, y):
    return y * jax.nn.sigmoid(acc + b)


def _ep_residual(acc, r):
    return acc + r


def _split3(x):
    hi = x.astype(BF16)
    r1 = x - hi.astype(F32)
    mid = r1.astype(BF16)
    lo = (r1 - mid.astype(F32)).astype(BF16)
    return hi, mid, lo


def _forget_cumsum_kernel(f_ref, b_ref, c_ref, hi_ref, mid_ref, lo_ref, carry_ref, *, bs):
    @pl.when(pl.program_id(0) == 0)
    def _():
        carry_ref[...] = jnp.zeros_like(carry_ref)

    z = f_ref[...] + b_ref[...]
    logf = jnp.minimum(z, 0.0) - jnp.log1p(jnp.exp(-jnp.abs(z)))
    row = lax.broadcasted_iota(I32, (bs, bs), 0)
    col = lax.broadcasted_iota(I32, (bs, bs), 1)
    tri = jnp.where(col <= row, 1.0, 0.0).astype(BF16)
    hi, mid, lo = _split3(logf)
    cs = (jnp.dot(tri, hi, preferred_element_type=F32) + jnp.dot(tri, mid, preferred_element_type=F32)
          + jnp.dot(tri, lo, preferred_element_type=F32)) + carry_ref[...]
    carry_ref[...] = cs[bs - 1:bs, :]
    c2 = cs * LOG2E
    c_ref[...] = c2
    hi_ref[...], mid_ref[...], lo_ref[...] = _split3(c2)


def _forget_cumsum(f_logit, b_forget_pad, bs=256):
    s, w = f_logit.shape
    bs = min(bs, s)
    blk = lambda: pl.BlockSpec((bs, w), lambda i: (i, 0))
    return pl.pallas_call(
        functools.partial(_forget_cumsum_kernel, bs=bs),
        grid=(s // bs,),
        in_specs=[blk(), pl.BlockSpec((1, w), lambda i: (0, 0))],
        out_specs=[blk(), blk(), blk(), blk()],
        out_shape=[jax.ShapeDtypeStruct((s, w), F32)] + [jax.ShapeDtypeStruct((s, w), BF16)] * 3,
        scratch_shapes=[pltpu.VMEM((1, w), F32)],
        compiler_params=_params("arbitrary"),
        name="forget_cumsum",
    )(f_logit, b_forget_pad)


def _fox_kernel(jstart_ref, qt_ref, qb_ref, k_ref, kb_ref, vt_ref, o_ref, m_ref, l_ref, acc_ref, *, nb, blk):
    h, i = pl.program_id(0), pl.program_id(1)
    qt = jnp.concatenate([qt_ref[...], qb_ref[...]], axis=0)

    def rows(j):
        return pl.ds(pl.multiple_of(j * blk, blk), blk)

    def scores(j):
        kj = jnp.concatenate([k_ref[rows(j), :], kb_ref[rows(j), :]], axis=1)
        return jnp.dot(kj, qt, preferred_element_type=F32)

    st = scores(i)
    kpos = lax.broadcasted_iota(I32, st.shape, 0)
    qpos = lax.broadcasted_iota(I32, st.shape, 1)
    st = jnp.where(kpos <= qpos, st, -jnp.inf)
    m0 = jnp.max(st, axis=0, keepdims=True)
    p = jnp.exp2(st - m0)
    m_ref[...] = m0
    l_ref[...] = jnp.sum(p, axis=0, keepdims=True)
    acc_ref[...] = jnp.dot(vt_ref[:, rows(i)], p.astype(BF16), preferred_element_type=F32)

    def update(state, j, st):
        m_old, l_old, acc_old = state
        m_new = jnp.maximum(m_old, jnp.max(st, axis=0, keepdims=True))
        p = jnp.exp2(st - m_new)
        corr = jnp.exp2(m_old - m_new)
        return (m_new, l_old * corr + jnp.sum(p, axis=0, keepdims=True),
                acc_old * corr + jnp.dot(vt_ref[:, rows(j)], p.astype(BF16), preferred_element_type=F32))

    def visit(*blocks):
        sts = [scores(j) for j in blocks]
        state = (m_ref[...], l_ref[...], acc_ref[...])
        for j, st in zip(blocks, sts):
            state = update(state, j, st)
        m_ref[...], l_ref[...], acc_ref[...] = state

    def quad(t, carry):
        j = i - 1 - 4 * t
        visit(j, j - 1, j - 2, j - 3)
        return carry

    n_prev = i - jstart_ref[h * nb + i]
    lax.fori_loop(0, n_prev // 4, quad, 0)
    j_rest = i - 1 - (n_prev // 4) * 4

    @pl.when(n_prev % 4 >= 2)
    def _():
        visit(j_rest, j_rest - 1)

    @pl.when(n_prev % 2 == 1)
    def _():
        visit(i - n_prev)
    o_ref[...] = (acc_ref[...] * (1.0 / l_ref[...])).T.astype(o_ref.dtype)


def _fox_attention(qt, k, vt, c2, c_pieces, logit_bound, n_heads, dh, blk=512):
    s = k.shape[0]
    blk = min(blk, s)
    nb = s // blk
    pieces = jnp.stack([jnp.transpose(p[:, :n_heads]) for p in c_pieces], axis=1)
    ones = jnp.ones_like(pieces)
    qb = jnp.pad(jnp.concatenate([pieces, ones], axis=1), ((0, 0), (0, dh - 6), (0, 0)))
    kb = jnp.pad(jnp.transpose(jnp.concatenate([ones, -pieces], axis=1), (0, 2, 1)),
                 ((0, 0), (0, 0), (0, dh - 6)))

    ct = jnp.transpose(c2[:, :n_heads]).reshape(n_heads, nb, blk)
    max_bias = ct[:, :, None, 0] - ct[:, None, :, blk - 1]
    ii = jnp.arange(nb)
    needed = jnp.logical_and(max_bias >= -(2.0 * logit_bound + EXP2_UNDERFLOW), ii[None, None, :] <= ii[None, :, None])
    jstart = (ii[None, :] + 1 - jnp.sum(needed, axis=-1)).astype(I32).reshape(n_heads * nb)

    return pl.pallas_call(
        functools.partial(_fox_kernel, nb=nb, blk=blk),
        grid_spec=pltpu.PrefetchScalarGridSpec(
            num_scalar_prefetch=1,
            grid=(n_heads, nb),
            in_specs=[
                pl.BlockSpec((dh, blk), lambda h, i, js: (h, i)),
                pl.BlockSpec((None, dh, blk), lambda h, i, js: (h, 0, i)),
                pl.BlockSpec((s, dh), lambda h, i, js: (0, h)),
                pl.BlockSpec((None, s, dh), lambda h, i, js: (h, 0, 0)),
                pl.BlockSpec((dh, s), lambda h, i, js: (h, 0)),
            ],
            out_specs=pl.BlockSpec((blk, dh), lambda h, i, js: (i, h)),
            scratch_shapes=[pltpu.VMEM((1, blk), F32), pltpu.VMEM((1, blk), F32), pltpu.VMEM((dh, blk), F32)],
        ),
        out_shape=jax.ShapeDtypeStruct((s, n_heads * dh), BF16),
        compiler_params=_params("parallel", "arbitrary"),
        name="fox_attention",
    )(jstart, qt, qb, k, kb, vt)


def _s5_params_kernel(lre_ref, lim_ref, ls_ref, bre_ref, bim_ref, ctre_ref, ctim_ref, cre_ref, cim_ref,
                      zre_ref, zim_ref, wre_ref, wim_ref, m_ref, alre_ref, alim_ref, *, chunk, n_ch):
    lre, lim = lre_ref[...], lim_ref[...]
    step = jnp.exp(ls_ref[...])
    ar, ai = lre * step, lim * step
    lc = chunk * n_ch
    tau = (lax.broadcasted_iota(I32, (1, 1, lc), 2) // n_ch).astype(F32)

    def cexp(t):
        mag = jnp.exp(ar * t)
        return mag * jnp.cos(ai * t), mag * jnp.sin(ai * t)

    e1r, e1i = cexp(1.0)
    nr, ni = e1r - 1.0, e1i
    den = lre * lre + lim * lim
    cr, ci = (nr * lre + ni * lim) / den, (ni * lre - nr * lim) / den
    bre, bim = bre_ref[...], bim_ref[...]
    bbr, bbi = cr * bre - ci * bim, cr * bim + ci * bre
    er, ei = cexp(tau)
    zr, zi = er * bbr - ei * bbi, er * bbi + ei * bbr
    zre_ref[...] = zr
    zim_ref[...] = zi
    e1r_t, e1i_t = cexp(tau + 1.0)
    ctre, ctim = ctre_ref[...], ctim_ref[...]
    wre_ref[...] = ctre * e1r_t - ctim * e1i_t
    wim_ref[...] = ctre * e1i_t + ctim * e1r_t
    dn = (((2,), (1,)), ((0,), (0,)))
    m_ref[...] = (lax.dot_general(cre_ref[...], zr, dn, precision=HIGHEST, preferred_element_type=F32)
                  - lax.dot_general(cim_ref[...], zi, dn, precision=HIGHEST, preferred_element_type=F32))
    alr, ali = cexp(float(chunk))
    alre_ref[...] = alr
    alim_ref[...] = ali


def _s5_operators(lam_re, lam_im, log_step, b_re, b_im, c_re, c_im, chunk):
    g, n = lam_re.shape
    n_ch = b_re.shape[-1]
    gs = LANES // n_ch
    assert g % gs == 0
    nsl, lc = g // gs, chunk * n_ch
    tile = lambda a: jnp.tile(a, (1, 1, chunk))
    ct_re, ct_im = jnp.transpose(c_re, (0, 2, 1)), jnp.transpose(c_im, (0, 2, 1))
    spec3 = lambda a, b: pl.BlockSpec((gs, a, b), lambda i: (i, 0, 0))
    zre, zim, wre, wim, mk, alre, alim = pl.pallas_call(
        functools.partial(_s5_params_kernel, chunk=chunk, n_ch=n_ch),
        grid=(nsl,),
        in_specs=[spec3(n, 1), spec3(n, 1), spec3(1, 1), spec3(n, lc), spec3(n, lc), spec3(n, lc), spec3(n, lc),
                  spec3(n_ch, n), spec3(n_ch, n)],
        out_specs=[spec3(n, lc), spec3(n, lc), spec3(n, lc), spec3(n, lc), spec3(n_ch, lc), spec3(n, 1), spec3(n, 1)],
        out_shape=[jax.ShapeDtypeStruct((g, n, lc), F32)] * 4 + [jax.ShapeDtypeStruct((g, n_ch, lc), F32)]
        + [jax.ShapeDtypeStruct((g, n, 1), F32)] * 2,
        compiler_params=_params("parallel"),
        name="s5_params",
    )(lam_re.reshape(g, n, 1), lam_im.reshape(g, n, 1), log_step.reshape(g, 1, 1), tile(b_re), tile(b_im),
      tile(ct_re), tile(ct_im), c_re, c_im)

    eye = jnp.eye(gs, dtype=F32)
    m5 = mk.reshape(nsl, gs, n_ch, chunk, n_ch)
    kk = jnp.transpose(m5, (0, 3, 1, 4, 2))
    kk = (kk[:, :, :, :, None, :] * eye[None, None, :, None, :, None]).reshape(nsl, chunk, LANES, LANES)
    d = jnp.arange(chunk)[None, :] - jnp.arange(chunk)[:, None]
    tm = jnp.where((d >= 0)[None, :, :, None, None], kk[:, jnp.clip(d, 0, chunk - 1)], 0.0)
    t_op = jnp.transpose(tm, (0, 1, 3, 2, 4)).reshape(nsl, chunk * LANES, chunk * LANES).astype(BF16)

    def inject(z):
        z5 = z.reshape(nsl, gs, n, chunk, n_ch)[:, :, :, ::-1, :]
        pd = jnp.transpose(z5, (0, 3, 1, 4, 2))
        return (pd[:, :, :, :, None, :] * eye[None, None, :, None, :, None]).reshape(nsl, chunk * LANES, gs * n)

    def readout(w):
        w5 = w.reshape(nsl, gs, n, chunk, n_ch)
        return (w5[:, :, :, :, None, :] * eye[None, :, None, None, :, None]).reshape(nsl, gs * n, chunk * LANES)

    p_op = jnp.concatenate([inject(zre), inject(zim)], axis=-1).astype(BF16)
    q_op = jnp.concatenate([readout(wre), -readout(wim)], axis=1).astype(BF16)
    a_l = jnp.concatenate([alre.reshape(nsl, 1, gs * n), alim.reshape(nsl, 1, gs * n)], axis=-1)
    return t_op, p_op, q_op, a_l


def _gelu_tanh(x):
    return 0.5 * x * (1.0 + jnp.tanh(math.sqrt(2.0 / math.pi) * (x + 0.044715 * (x * x * x))))


def _s5_conv_kernel(u_ref, t_ref, p_ref, q_ref, a_ref, d_ref, o_ref, hst_ref, s_ref, hp_ref, *, chunk, bk):
    hn = a_ref.shape[1] // 2

    @pl.when(pl.program_id(1) == 0)
    def _():
        hst_ref[...] = jnp.zeros_like(hst_ref)

    x = jnp.concatenate([u_ref[pl.ds(s, bk, stride=chunk), :] for s in range(chunk)], axis=1).astype(BF16)
    s_ref[...] = jnp.dot(x, p_ref[...], preferred_element_type=F32)
    ar, ai = a_ref[:, :hn], a_ref[:, hn:]

    def body(k, h):
        hr, hi = h
        hp_ref[pl.ds(k, 1), :hn] = hr
        hp_ref[pl.ds(k, 1), hn:] = hi
        row = s_ref[pl.ds(k, 1), :]
        return ar * hr - ai * hi + row[:, :hn], ar * hi + ai * hr + row[:, hn:]

    hr, hi = lax.fori_loop(0, bk, body, (hst_ref[:, :hn], hst_ref[:, hn:]))
    hst_ref[:, :hn] = hr
    hst_ref[:, hn:] = hi

    y = (jnp.dot(x, t_ref[...], preferred_element_type=F32)
         + jnp.dot(hp_ref[...].astype(BF16), q_ref[...], preferred_element_type=F32))
    d = d_ref[...]
    for t in range(chunk):
        yt = y[:, t * LANES:(t + 1) * LANES] + d * u_ref[pl.ds(t, bk, stride=chunk), :]
        o_ref[pl.ds(t, bk, stride=chunk), :] = _gelu_tanh(yt)


def _s5_conv(u, t_op, p_op, q_op, a_l, d_skip, chunk, bk=512):
    s, p = u.shape
    nsl = p // LANES
    bk = min(bk, s // chunk)
    bm = bk * chunk
    hn2 = a_l.shape[-1]
    return pl.pallas_call(
        functools.partial(_s5_conv_kernel, chunk=chunk, bk=bk),
        grid=(nsl, s // bm),
        in_specs=[
            pl.BlockSpec((bm, LANES), lambda sl, i: (i, sl)),
            pl.BlockSpec((None, chunk * LANES, chunk * LANES), lambda sl, i: (sl, 0, 0)),
            pl.BlockSpec((None, chunk * LANES, hn2), lambda sl, i: (sl, 0, 0)),
            pl.BlockSpec((None, hn2, chunk * LANES), lambda sl, i: (sl, 0, 0)),
            pl.BlockSpec((None, 1, hn2), lambda sl, i: (sl, 0, 0)),
            pl.BlockSpec((1, LANES), lambda sl, i: (0, sl)),
        ],
        out_specs=pl.BlockSpec((bm, LANES), lambda sl, i: (i, sl)),
        out_shape=jax.ShapeDtypeStruct((s, p), F32),
        scratch_shapes=[pltpu.VMEM((1, hn2), F32), pltpu.VMEM((bk, hn2), F32), pltpu.VMEM((bk, hn2), F32)],
        compiler_params=_params("parallel", "arbitrary"),
        name="s5_conv",
    )(u, t_op, p_op, q_op, a_l, d_skip.reshape(1, p))


def _mem_attn_kernel(q_ref, k_ref, v_ref, o_ref, *, n_heads, dm):
    for h in range(n_heads):
        sl = slice(h * dm, (h + 1) * dm)
        s = lax.dot_general(q_ref[:, sl], k_ref[:, sl], (((1,), (1,)), ((), ())), preferred_element_type=F32)
        p = jnp.exp(s - jnp.max(s, axis=-1, keepdims=True))
        o = jnp.dot(p.astype(BF16), v_ref[:, sl], preferred_element_type=F32) / jnp.sum(p, axis=-1, keepdims=True)
        o_ref[:, sl] = o.astype(o_ref.dtype)


def _mem_attention(q, k, v, n_heads, dm, bm=512):
    s, w = q.shape
    m = k.shape[0]
    bm = min(bm, s)
    return pl.pallas_call(
        functools.partial(_mem_attn_kernel, n_heads=n_heads, dm=dm),
        grid=(s // bm,),
        in_specs=[pl.BlockSpec((bm, w), lambda i: (i, 0)), pl.BlockSpec((m, w), lambda i: (0, 0)),
                  pl.BlockSpec((m, w), lambda i: (0, 0))],
        out_specs=pl.BlockSpec((bm, w), lambda i: (i, 0)),
        out_shape=jax.ShapeDtypeStruct((s, w), BF16),
        compiler_params=_params("parallel"),
        name="mem_attention",
    )(q, k, v)


def _merge_kernel(ya_ref, ys_ref, ym_ref, wa_ref, ws_ref, wm_ref, ga_ref, gs_ref, gm_ref, o_ref):
    a = jnp.dot(ya_ref[...], wa_ref[...], preferred_element_type=F32)
    s = jnp.dot(ys_ref[...], ws_ref[...], preferred_element_type=F32)
    m = jnp.dot(ym_ref[...], wm_ref[...], preferred_element_type=F32)
    o_ref[...] = (ga_ref[...].astype(F32) * a + gs_ref[...].astype(F32) * s
                  + gm_ref[...].astype(F32) * m).astype(o_ref.dtype)


def _merge(ya, ys, ym, wa, ws, wm, gates, bm=1024, bn=512):
    s, d = ya.shape[0], wa.shape[1]
    bm, bn = min(bm, s), min(bn, d)
    nb = d // bn
    row = lambda a: pl.BlockSpec((bm, a.shape[1]), lambda i, j: (i, 0))
    col = lambda a: pl.BlockSpec((a.shape[0], bn), lambda i, j: (0, j))
    gate = lambda b: pl.BlockSpec((bm, bn), lambda i, j, b=b: (i, j + b * nb))
    return pl.pallas_call(
        _merge_kernel,
        grid=(s // bm, nb),
        in_specs=[row(ya), row(ys), row(ym), col(wa), col(ws), col(wm), gate(0), gate(1), gate(2)],
        out_specs=pl.BlockSpec((bm, bn), lambda i, j: (i, j)),
        out_shape=jax.ShapeDtypeStruct((s, d), BF16),
        compiler_params=_params("parallel", "parallel"),
        name="merge",
    )(ya, ys, ym, wa, ws, wm, gates, gates, gates)


def _router_kernel(x_ref, g_ref, wr_ref, br_ref, xp_ref, idx_ref, gate_ref, rank_ref, cnt_ref, carry_ref, *, bm):
    @pl.when(pl.program_id(0) == 0)
    def _():
        carry_ref[...] = jnp.zeros_like(carry_ref)

    x = x_ref[...]
    ms = jnp.mean(x * x, axis=-1, keepdims=True)
    xn = x * lax.rsqrt(ms + RMS_EPS) * g_ref[...]
    d2 = xn.shape[1] // 2
    xp_ref[...] = _pack_bf16_pair(xn[:, :d2], xn[:, d2:])

    n_exp = wr_ref.shape[1]
    x_hi, w_hi = xn.astype(BF16), wr_ref[...].astype(BF16)
    x_lo = (xn - x_hi.astype(F32)).astype(BF16)
    w_lo = (wr_ref[...] - w_hi.astype(F32)).astype(BF16)
    logits = (jnp.dot(x_hi, w_hi, preferred_element_type=F32) + jnp.dot(x_lo, w_hi, preferred_element_type=F32)
              + jnp.dot(x_hi, w_lo, preferred_element_type=F32)) + br_ref[...]
    lane = lax.broadcasted_iota(I32, logits.shape, 1)
    kcol = lax.broadcasted_iota(I32, (bm, TOP_K), 1)
    work = logits
    member = jnp.zeros_like(logits)
    idxs, vals = [], []
    for _ in range(TOP_K):
        mx = jnp.max(work, axis=-1, keepdims=True)
        ik = jnp.min(jnp.where(work == mx, lane, n_exp), axis=-1, keepdims=True)
        sel = lane == ik
        member = jnp.where(sel, 1.0, member)
        work = jnp.where(sel, -jnp.inf, work)
        idxs.append(ik)
        vals.append(mx)
    es = [jnp.exp(v - vals[0]) for v in vals]
    den = es[0] + es[1] + es[2] + es[3]

    r = lax.broadcasted_iota(I32, (bm, bm), 0)
    c = lax.broadcasted_iota(I32, (bm, bm), 1)
    tri = jnp.where(c < r, 1.0, 0.0).astype(BF16)
    prefix = jnp.dot(tri, member.astype(BF16), preferred_element_type=F32) + carry_ref[...]
    total = carry_ref[...] + jnp.sum(member, axis=0, keepdims=True)
    carry_ref[...] = total
    cnt_ref[...] = total

    idx_o = jnp.zeros((bm, TOP_K), I32)
    gate_o = jnp.zeros((bm, TOP_K), F32)
    rank_o = jnp.zeros((bm, TOP_K), F32)
    for k in range(TOP_K):
        rk = jnp.sum(jnp.where(lane == idxs[k], prefix, 0.0), axis=-1, keepdims=True)
        idx_o = jnp.where(kcol == k, idxs[k], idx_o)
        gate_o = jnp.where(kcol == k, es[k] / den, gate_o)
        rank_o = jnp.where(kcol == k, rk, rank_o)
    idx_ref[...] = idx_o
    gate_ref[...] = gate_o
    rank_ref[...] = rank_o.astype(I32)


def _router(x1, g, w_router, b_router, bm=256):
    t, d = x1.shape
    n_exp = w_router.shape[1]
    bm = min(bm, t)
    small = lambda: pl.BlockSpec((bm, TOP_K), lambda i: (i, 0))
    return pl.pallas_call(
        functools.partial(_router_kernel, bm=bm),
        grid=(t // bm,),
        in_specs=[pl.BlockSpec((bm, d), lambda i: (i, 0)), pl.BlockSpec((1, d), lambda i: (0, 0)),
                  pl.BlockSpec((d, n_exp), lambda i: (0, 0)), pl.BlockSpec((1, n_exp), lambda i: (0, 0))],
        out_specs=[pl.BlockSpec((bm, d // 2), lambda i: (i, 0)), small(), small(), small(),
                   pl.BlockSpec((1, n_exp), lambda i: (0, 0))],
        out_shape=[jax.ShapeDtypeStruct((t, d // 2), U32), jax.ShapeDtypeStruct((t, TOP_K), I32),
                   jax.ShapeDtypeStruct((t, TOP_K), F32), jax.ShapeDtypeStruct((t, TOP_K), I32),
                   jax.ShapeDtypeStruct((1, n_exp), F32)],
        scratch_shapes=[pltpu.VMEM((1, n_exp), F32)],
        compiler_params=_params("arbitrary"),
        name="router",
    )(x1, g.reshape(1, d), w_router, b_router.reshape(1, n_exp))


def _dispatch_kernel(fill_ref, dest_ref, xp_ref, xs_ref, zero_ref, sem_z, sem_r, *, bm, bm_e, n_tiles):
    @pl.when(pl.program_id(0) == 0)
    def _():
        zero_ref[...] = jnp.zeros_like(zero_ref)

        def fill(t, wait):
            @pl.when(fill_ref[t] > 0)
            def _():
                cp = pltpu.make_async_copy(zero_ref, xs_ref.at[pl.ds(t * bm_e, bm_e), :], sem_z)
                cp.wait() if wait else cp.start()

        lax.fori_loop(0, n_tiles, lambda t, c: (fill(t, False), c)[1], 0)
        lax.fori_loop(0, n_tiles, lambda t, c: (fill(t, True), c)[1], 0)

    def row_copy(r, k):
        return pltpu.make_async_copy(xp_ref.at[pl.ds(r, 1), :], xs_ref.at[pl.ds(dest_ref[r * TOP_K + k], 1), :], sem_r)

    def issue(r, c):
        for k in range(TOP_K):
            row_copy(r, k).start()
        return c

    def drain(r, c):
        for k in range(TOP_K):
            row_copy(r, k).wait()
        return c

    lax.fori_loop(0, bm, issue, 0)
    lax.fori_loop(0, bm, drain, 0)


def _dispatch(xp, dest_flat, tile_fill, bm_e, bm=256):
    t, d2 = xp.shape
    bm = min(bm, t)
    n_tiles = tile_fill.shape[0]
    n_rows = n_tiles * bm_e
    return pl.pallas_call(
        functools.partial(_dispatch_kernel, bm=bm, bm_e=bm_e, n_tiles=n_tiles),
        grid_spec=pltpu.PrefetchScalarGridSpec(
            num_scalar_prefetch=1,
            grid=(t // bm,),
            in_specs=[pl.BlockSpec((bm * TOP_K,), lambda i, lt: (i,), memory_space=pltpu.SMEM),
                      pl.BlockSpec((bm, d2), lambda i, lt: (i, 0))],
            out_specs=pl.BlockSpec(memory_space=pl.ANY),
            scratch_shapes=[pltpu.VMEM((bm_e, d2), U32), pltpu.SemaphoreType.DMA(()), pltpu.SemaphoreType.DMA(())],
        ),
        out_shape=jax.ShapeDtypeStruct((n_rows, d2), U32),
        compiler_params=_params("arbitrary"),
        name="moe_dispatch",
    )(tile_fill, dest_flat, xp)


FIRST_VISIT, TILE_USED = 1, 2


def _work_list(tile_start, tile_count, n_used, n_tiles, n_chunks):
    n_exp = tile_start.shape[0]
    w = jnp.arange(n_tiles * n_chunks, dtype=I32)
    e = jnp.minimum(jnp.sum((n_chunks * (tile_start + tile_count))[None, :] <= w[:, None], axis=1), n_exp - 1).astype(I32)
    cnt = jnp.maximum(tile_count[e], 1)
    r = w - n_chunks * tile_start[e]
    used = w < n_chunks * n_used
    n_unused = jnp.maximum(n_tiles - n_used, 1)
    r2 = w - n_chunks * n_used
    tile = jnp.where(used, tile_start[e] + r % cnt, n_used + r2 % n_unused)
    chunk = jnp.where(used, r // cnt, r2 // n_unused)
    last = n_chunks * n_used - 1
    in_tile = jnp.where(used, tile, tile[last])
    w_exp = jnp.where(used, e, e[last])
    w_chunk = jnp.where(used, chunk, chunk[last])
    flags = jnp.where(used, TILE_USED + jnp.where(r % cnt == 0, FIRST_VISIT, 0), 0)
    return [a.astype(I32) for a in (tile, chunk, in_tile, w_exp, w_chunk, flags)]


def _moe_up_kernel(tile_ref, chunk_ref, xin_ref, we_ref, wc_ref, flag_ref, xs_ref, wg_ref, wl_ref, bg_ref, bl_ref,
                   o_ref, wb_ref):
    flag = flag_ref[pl.program_id(0)]
    tf = wg_ref.shape[1]

    @pl.when((flag & FIRST_VISIT) != 0)
    def _():
        wb_ref[:, :tf] = wg_ref[...].astype(BF16)
        wb_ref[:, tf:] = wl_ref[...].astype(BF16)

    @pl.when((flag & TILE_USED) != 0)
    def _():
        lo, hi = _unpack_bf16_pair(xs_ref[...])
        xlo, xhi = lo.astype(BF16), hi.astype(BF16)
        d2 = xlo.shape[1]
        hcat = (jnp.dot(xlo, wb_ref[:d2, :], preferred_element_type=F32)
                + jnp.dot(xhi, wb_ref[d2:, :], preferred_element_type=F32))
        glu = jnp.minimum(hcat[:, :tf] + bg_ref[...], SWIGLU_LIMIT)
        lin = jnp.clip(hcat[:, tf:] + bl_ref[...], -SWIGLU_LIMIT, SWIGLU_LIMIT)
        o_ref[...] = (glu * jax.nn.sigmoid(SWIGLU_ALPHA * glu) * (lin + 1.0)).astype(o_ref.dtype)

    @pl.when((flag & TILE_USED) == 0)
    def _():
        o_ref[...] = jnp.zeros_like(o_ref)


def _moe_up(xs, work, w1, b1, bm_e, tf):
    n_rows, d2 = xs.shape
    n_exp, d, f2 = w1.shape
    ff = f2 // 2
    nf = ff // tf
    wspec = lambda off: pl.BlockSpec((None, d, tf), lambda w, t, c, xi, we, wc, fl: (we[w], 0, wc[w] + off))
    bspec = lambda off: pl.BlockSpec((None, 1, tf), lambda w, t, c, xi, we, wc, fl: (we[w], 0, wc[w] + off))
    return pl.pallas_call(
        _moe_up_kernel,
        grid_spec=pltpu.PrefetchScalarGridSpec(
            num_scalar_prefetch=6,
            grid=(work[0].shape[0],),
            in_specs=[pl.BlockSpec((bm_e, d2), lambda w, t, c, xi, we, wc, fl: (xi[w], 0)),
                      wspec(0), wspec(nf), bspec(0), bspec(nf)],
            out_specs=pl.BlockSpec((bm_e, tf), lambda w, t, c, xi, we, wc, fl: (t[w], c[w])),
            scratch_shapes=[pltpu.VMEM((d, 2 * tf), BF16)],
        ),
        out_shape=jax.ShapeDtypeStruct((n_rows, ff), BF16),
        compiler_params=_params("arbitrary"),
        name="moe_up",
    )(*work, xs, w1, w1, b1.reshape(n_exp, 1, f2), b1.reshape(n_exp, 1, f2))


def _moe_down_kernel(tile_ref, chunk_ref, hin_ref, we_ref, wc_ref, flag_ref, h_ref, w2_ref, b2_ref, o_ref, w2b_ref):
    flag = flag_ref[pl.program_id(0)]

    @pl.when((flag & FIRST_VISIT) != 0)
    def _():
        w2b_ref[...] = w2_ref[...].astype(BF16)

    @pl.when((flag & TILE_USED) != 0)
    def _():
        y = jnp.dot(h_ref[...], w2b_ref[...], preferred_element_type=F32) + b2_ref[...]
        half = y.shape[1] // 2
        o_ref[...] = _pack_bf16_pair(y[:, :half], y[:, half:])

    @pl.when((flag & TILE_USED) == 0)
    def _():
        o_ref[...] = jnp.zeros_like(o_ref)


def _moe_down(hid, work, w2, b2, bm_e, tn):
    n_rows, ff = hid.shape
    n_exp, _, d = w2.shape
    return pl.pallas_call(
        _moe_down_kernel,
        grid_spec=pltpu.PrefetchScalarGridSpec(
            num_scalar_prefetch=6,
            grid=(work[0].shape[0],),
            in_specs=[pl.BlockSpec((bm_e, ff), lambda w, t, c, hi, we, wc, fl: (hi[w], 0)),
                      pl.BlockSpec((None, ff, tn), lambda w, t, c, hi, we, wc, fl: (we[w], 0, wc[w])),
                      pl.BlockSpec((None, 1, tn), lambda w, t, c, hi, we, wc, fl: (we[w], 0, wc[w]))],
            out_specs=pl.BlockSpec((bm_e, tn // 2), lambda w, t, c, hi, we, wc, fl: (t[w], c[w])),
            scratch_shapes=[pltpu.VMEM((ff, tn), BF16)],
        ),
        out_shape=jax.ShapeDtypeStruct((n_rows, d // 2), U32),
        compiler_params=_params("arbitrary"),
        name="moe_down",
    )(*work, hid, w2, b2.reshape(n_exp, 1, d))


def _combine_kernel(dest_ref, x_ref, gate_ref, ys_ref, o_ref, buf_ref, sem, *, bm, tn):
    def row_copy(r, k):
        return pltpu.make_async_copy(ys_ref.at[pl.ds(dest_ref[r * TOP_K + k], 1), :],
                                     buf_ref.at[k, pl.ds(r, 1), :], sem)

    def issue(r, c):
        for k in range(TOP_K):
            row_copy(r, k).start()
        return c

    def drain(r, c):
        for k in range(TOP_K):
            row_copy(r, k).wait()
        return c

    lax.fori_loop(0, bm, issue, 0)
    lax.fori_loop(0, bm, drain, 0)
    g = gate_ref[...]
    half = tn // 2
    for n in range(x_ref.shape[1] // tn):
        acc_lo = x_ref[:, n * tn:n * tn + half]
        acc_hi = x_ref[:, n * tn + half:(n + 1) * tn]
        for k in range(TOP_K):
            lo, hi = _unpack_bf16_pair(buf_ref[k, :, n * half:(n + 1) * half])
            acc_lo = acc_lo + g[:, k:k + 1] * lo
            acc_hi = acc_hi + g[:, k:k + 1] * hi
        o_ref[:, n * tn:n * tn + half] = acc_lo
        o_ref[:, n * tn + half:(n + 1) * tn] = acc_hi


def _combine(x1, gate, dest_flat, ys, tn, bm=128):
    t, d = x1.shape
    bm = min(bm, t)
    return pl.pallas_call(
        functools.partial(_combine_kernel, bm=bm, tn=tn),
        grid=(t // bm,),
        in_specs=[pl.BlockSpec((bm * TOP_K,), lambda i: (i,), memory_space=pltpu.SMEM),
                  pl.BlockSpec((bm, d), lambda i: (i, 0)),
                  pl.BlockSpec((bm, TOP_K), lambda i: (i, 0)),
                  pl.BlockSpec(memory_space=pl.ANY)],
        out_specs=pl.BlockSpec((bm, d), lambda i: (i, 0)),
        out_shape=jax.ShapeDtypeStruct((t, d), F32),
        scratch_shapes=[pltpu.VMEM((TOP_K, bm, d // 2), U32), pltpu.SemaphoreType.DMA(())],
        compiler_params=_params("arbitrary"),
        name="moe_combine",
    )(dest_flat, x1, gate, ys)


def _mixer(x2, mem2, layer, norm_mix, norm_mem, w_in, b_gate, b_forget, q_norm_attn, k_norm_attn, q_norm_mem, k_norm_mem,
           w_mem_kv, lam_re, lam_im, log_step, ssm_b_re, ssm_b_im, ssm_c_re, ssm_c_im, ssm_d, w_glu, b_glu,
           w_up_attn, w_up_ssm, w_up_mem, w_out):
    s, d = x2.shape
    n_heads, dh = b_forget.shape[0], q_norm_attn.shape[0]
    aw = n_heads * dh
    pw = ssm_d.shape[0]
    dm = q_norm_mem.shape[0]
    mw = w_up_mem.shape[0]
    mem_heads = mw // dm
    o_k, o_v, o_f, o_u, o_qm, o_g = aw, 2 * aw, 3 * aw, 3 * aw + n_heads, 3 * aw + n_heads + pw, 3 * aw + n_heads + pw + mw
    bf = lambda a: a.astype(BF16)

    h = _rmsnorm(x2, norm_mix, BF16)
    w_qkv, w_f, w_rest = _realign_w_in(w_in, layer, o_f, o_u, o_g, n_heads)
    qt = _matmul(h, w_qkv, [(jnp.tile(q_norm_attn, n_heads).reshape(1, aw), "row", 0)],
                 functools.partial(_ep_headnorm, dh=dh, scale=dh ** -0.5 * LOG2E), BF16, 1024, 512, "in_q",
                 w_cols=(0, aw), transpose_out=True)
    k = _matmul(h, w_qkv, [(jnp.tile(k_norm_attn, n_heads).reshape(1, aw), "row", 0)],
                functools.partial(_ep_headnorm, dh=dh, scale=1.0), BF16, 1024, 512, "in_k", w_cols=(o_k, aw))
    vt = _matmul(h, w_qkv, [], _ep_plain, BF16, 1024, 512, "in_v", w_cols=(o_v, aw), transpose_out=True)
    f_logit = _matmul(h, w_f, [], _ep_plain, F32, 1024, LANES, "in_f")
    gates = _matmul(h, w_rest, [(b_gate.reshape(1, 3 * d), "row", 0)], _ep_sigmoid_bias, BF16, 1024, 1024,
                    "in_gates", w_cols=(0, 3 * d))
    u = _matmul(h, w_rest, [], _ep_plain, F32, 1024, 512, "in_u", w_cols=(3 * d, pw))
    qm = _matmul(h, w_rest, [(jnp.tile(q_norm_mem, mem_heads).reshape(1, mw), "row", 0)],
                 functools.partial(_ep_headnorm, dh=dm, scale=dm ** -0.5), BF16, 1024, 512, "in_qm",
                 w_cols=(3 * d + pw, mw))

    c2, *c_pieces = _forget_cumsum(f_logit, jnp.pad(b_forget, (0, LANES - n_heads)).reshape(1, LANES))
    logit_bound = 1.02 * LOG2E * dh ** 0.5 * jnp.max(jnp.abs(q_norm_attn)) * jnp.max(jnp.abs(k_norm_attn))
    y_attn = _fox_attention(qt, k, vt, c2, c_pieces, logit_bound, n_heads, dh)

    t_op, p_op, q_op, a_l = _s5_operators(lam_re, lam_im, log_step, ssm_b_re, ssm_b_im, ssm_c_re, ssm_c_im, SSM_CHUNK)
    yg = _s5_conv(u, t_op, p_op, q_op, a_l, ssm_d, SSM_CHUNK)
    y_ssm = _matmul(yg, bf(w_glu), [(b_glu.reshape(1, pw), "row", 0), (yg, "tile", 0)], _ep_glu, BF16, 1024, 512, "s5_glu")

    mem_h = _rmsnorm(mem2, norm_mem, BF16)
    k_m = _matmul(mem_h, bf(w_mem_kv[:, :mw]), [(jnp.tile(k_norm_mem, mem_heads).reshape(1, mw), "row", 0)],
                  functools.partial(_ep_headnorm, dh=dm, scale=1.0), BF16, 256, 512, "mem_k")
    v_m = _matmul(mem_h, bf(w_mem_kv[:, mw:]), [], _ep_plain, BF16, 256, 512, "mem_v")
    y_mem = _mem_attention(qm, k_m, v_m, mem_heads, dm)

    merged = _merge(y_attn, y_ssm, y_mem, bf(w_up_attn), bf(w_up_ssm), bf(w_up_mem), gates)
    return _matmul(merged, bf(w_out), [(x2, "tile", 0)], _ep_residual, F32, 1024, 512, "out_proj")


def _moe(x1, norm_ffn, w_router, b_router, exp_w1, exp_b1, exp_w2, exp_b2, bm_e=512):
    t, d = x1.shape
    n_exp = w_router.shape[1]
    xp, idx, gate, rank, counts = _router(x1, norm_ffn, w_router, b_router)

    counts = counts.reshape(n_exp).astype(I32)
    padded = (counts + bm_e - 1) // bm_e * bm_e
    pend = jnp.cumsum(padded)
    pstart = pend - padded
    dest_flat = (pstart[idx] + rank).reshape(t * TOP_K)
    n_tiles = -(-(t * TOP_K) // bm_e) + n_exp
    n_used = (pend[-1] // bm_e).astype(I32)
    tiles = jnp.arange(n_tiles, dtype=I32)
    has_pad = jnp.zeros((n_tiles,), I32).at[pend // bm_e - 1].max((padded > counts).astype(I32))
    tile_fill = jnp.where(tiles < n_used, has_pad, 1)
    tf = _pick_block(exp_w2.shape[1], MOE_UP_CHUNK)
    tn = _pick_block(d, MOE_DOWN_CHUNK)
    work_up = _work_list(pstart // bm_e, padded // bm_e, n_used, n_tiles, exp_w2.shape[1] // tf)
    work_down = _work_list(pstart // bm_e, padded // bm_e, n_used, n_tiles, d // tn)

    xs = _dispatch(xp, dest_flat, tile_fill, bm_e)
    hid = _moe_up(xs, work_up, exp_w1, exp_b1, bm_e, tf)
    ys = _moe_down(hid, work_down, exp_w2, exp_b2, bm_e, tn)
    return _combine(x1, gate, dest_flat, ys, tn)


def kernel(x, mem, norm_mix, norm_ffn, norm_mem, w_in, b_gate, b_forget, q_norm_attn, k_norm_attn, q_norm_mem, k_norm_mem, w_mem_kv, lam_re, lam_im, log_step, ssm_b_re, ssm_b_im, ssm_c_re, ssm_c_im, ssm_d, w_glu, b_glu, w_up_attn, w_up_ssm, w_up_mem, w_out, w_router, b_router, exp_w1, exp_b1, exp_w2, exp_b2):
    b, s, d = x.shape
    assert b == 1, "one sequence per call"
    x2 = x.reshape(s, d)
    for l in range(norm_mix.shape[0]):
        x2 = _mixer(x2, mem.reshape(mem.shape[1], d), l, norm_mix[l], norm_mem[l], w_in, b_gate[l], b_forget[l],
                    q_norm_attn[l], k_norm_attn[l], q_norm_mem[l], k_norm_mem[l], w_mem_kv[l], lam_re[l], lam_im[l],
                    log_step[l], ssm_b_re[l], ssm_b_im[l], ssm_c_re[l], ssm_c_im[l], ssm_d[l], w_glu[l], b_glu[l],
                    w_up_attn[l], w_up_ssm[l], w_up_mem[l], w_out[l])
        x2 = _moe(x2, norm_ffn[l], w_router[l], b_router[l], exp_w1[l], exp_b1[l], exp_w2[l], exp_b2[l])
    return x2.reshape(b, s, d)
```

```python
import functools
import math

import jax
import jax.numpy as jnp
from jax import lax
from jax.experimental import pallas as pl
from jax.experimental.pallas import tpu as pltpu

F32 = jnp.float32
BF16 = jnp.bfloat16
I32 = jnp.int32
U32 = jnp.uint32

RMS_EPS = 1e-6
TOP_K = 4
SWIGLU_ALPHA = 1.702
SWIGLU_LIMIT = 7.0

LANES = 128
VMEM_LIMIT = 56 * 1024 * 1024
SSM_CHUNK = 8
MOE_UP_CHUNK = 384
MOE_DOWN_CHUNK = 2048
HIGHEST = lax.Precision.HIGHEST
LOG2E = math.log2(math.e)
EXP2_UNDERFLOW = 160.0


def _pick_block(n, target, align=LANES):
    if n <= target:
        return n
    b = target // align * align
    while n % b:
        b -= align
    return b


def _pack_bf16_pair(lo, hi):
    a = pltpu.bitcast(lo.astype(BF16).astype(F32), U32) >> 16
    b = pltpu.bitcast(hi.astype(BF16).astype(F32), U32) & jnp.uint32(0xFFFF0000)
    return a | b


def _unpack_bf16_pair(w):
    return pltpu.bitcast(w << 16, F32), pltpu.bitcast(w & jnp.uint32(0xFFFF0000), F32)


def _params(*sem):
    return pltpu.CompilerParams(dimension_semantics=sem, vmem_limit_bytes=VMEM_LIMIT)


def _rmsnorm_kernel(x_ref, g_ref, o_ref):
    x = x_ref[...]
    ms = jnp.mean(x * x, axis=-1, keepdims=True)
    o_ref[...] = (x * lax.rsqrt(ms + RMS_EPS) * g_ref[...]).astype(o_ref.dtype)


def _rmsnorm(x, g, out_dtype, bm=256):
    m, d = x.shape
    bm = min(bm, m)
    return pl.pallas_call(
        _rmsnorm_kernel,
        grid=(m // bm,),
        in_specs=[pl.BlockSpec((bm, d), lambda i: (i, 0)), pl.BlockSpec((1, d), lambda i: (0, 0))],
        out_specs=pl.BlockSpec((bm, d), lambda i: (i, 0)),
        out_shape=jax.ShapeDtypeStruct((m, d), out_dtype),
        compiler_params=_params("parallel"),
        name="rmsnorm",
    )(x, g.reshape(1, d))


def _mm_kernel(*refs, n_extra, epilogue, transpose_out):
    x_ref, w_ref = refs[0], refs[1]
    extras = refs[2:2 + n_extra]
    o_ref = refs[2 + n_extra]
    acc = jnp.dot(x_ref[...].astype(BF16), w_ref[...], preferred_element_type=F32)
    out = epilogue(acc, *[e[...] for e in extras])
    o_ref[...] = (out.T if transpose_out else out).astype(o_ref.dtype)


def _matmul(x, w, extras, epilogue, out_dtype, bm, bn, name, w_cols=None, transpose_out=False):
    m, k = x.shape
    c0, n = w_cols if w_cols is not None else (0, w.shape[1])
    bm, bn = _pick_block(m, bm, 8), _pick_block(n, bn)
    while c0 % bn or n % bn:
        bn -= LANES
    wj = c0 // bn
    in_specs = [pl.BlockSpec((bm, k), lambda i, j: (i, 0)), pl.BlockSpec((k, bn), lambda i, j: (0, j + wj))]
    args = [x, w]
    for arr, kind, off in extras:
        if kind == "row":
            in_specs.append(pl.BlockSpec((1, bn), lambda i, j, off=off: (0, j + off)))
        else:
            in_specs.append(pl.BlockSpec((bm, bn), lambda i, j, off=off: (i, j + off)))
        args.append(arr)
    if transpose_out:
        out_spec, out_shape = pl.BlockSpec((bn, bm), lambda i, j: (j, i)), (n, m)
    else:
        out_spec, out_shape = pl.BlockSpec((bm, bn), lambda i, j: (i, j)), (m, n)
    return pl.pallas_call(
        functools.partial(_mm_kernel, n_extra=len(extras), epilogue=epilogue, transpose_out=transpose_out),
        grid=(m // bm, n // bn),
        in_specs=in_specs,
        out_specs=out_spec,
        out_shape=jax.ShapeDtypeStruct(out_shape, out_dtype),
        compiler_params=_params("parallel", "parallel"),
        name=name,
    )(*args)


def _w_in_cols_kernel(w_hbm, o_ref, buf_ref, sem, *, layer, col0, bn, n_valid):
    j, n_steps = pl.program_id(0), pl.num_programs(0)

    def slab(step, slot):
        return pltpu.make_async_copy(w_hbm.at[pl.ds(col0 + step * bn, bn), pl.ds(layer, 1), :],
                                     buf_ref.at[slot], sem.at[slot])

    @pl.when(j == 0)
    def _():
        slab(0, 0).start()

    @pl.when(j + 1 < n_steps)
    def _():
        slab(j + 1, (j + 1) % 2).start()

    slab(j, j % 2).wait()
    cols = buf_ref[j % 2, :, 0, :].T
    if n_valid < bn:
        lane = lax.broadcasted_iota(I32, cols.shape, 1)
        cols = jnp.where(lane < n_valid, cols, 0.0)
    o_ref[...] = cols.astype(BF16)


def _w_in_cols(w_t, layer, col0, n_cols, name, n_valid=None, bn=LANES):
    n_valid = n_cols if n_valid is None else n_valid
    d = w_t.shape[2]
    assert n_cols % bn == 0
    return pl.pallas_call(
        functools.partial(_w_in_cols_kernel, layer=layer, col0=col0, bn=bn, n_valid=min(n_valid, bn)),
        grid=(n_cols // bn,),
        in_specs=[pl.BlockSpec(memory_space=pl.ANY)],
        out_specs=pl.BlockSpec((d, bn), lambda j: (0, j)),
        out_shape=jax.ShapeDtypeStruct((d, n_cols), BF16),
        scratch_shapes=[pltpu.VMEM((2, bn, 1, d), F32), pltpu.SemaphoreType.DMA((2,))],
        compiler_params=_params("arbitrary"),
        name=name,
    )(w_t)


def _ep_plain(acc):
    return acc


def _ep_headnorm(acc, gain, *, dh, scale):
    outs = []
    for h in range(acc.shape[1] // dh):
        blk = acc[:, h * dh:(h + 1) * dh]
        ms = jnp.mean(blk * blk, axis=-1, keepdims=True)
        outs.append(blk * lax.rsqrt(ms + RMS_EPS) * gain[:, h * dh:(h + 1) * dh] * scale)
    return outs[0] if len(outs) == 1 else jnp.concatenate(outs, axis=1)


def _ep_sigmoid_bias(acc, b):
    return jax.nn.sigmoid(acc + b)


def _ep_glu(acc, b, y):
    return y * jax.nn.sigmoid(acc + b)


def _ep_residual(acc, r):
    return acc + r


def _split3(x):
    hi = x.astype(BF16)
    r1 = x - hi.astype(F32)
    mid = r1.astype(BF16)
    lo = (r1 - mid.astype(F32)).astype(BF16)
    return hi, mid, lo


def _forget_cumsum_kernel(f_ref, b_ref, c_ref, hi_ref, mid_ref, lo_ref, carry_ref, *, bs):
    @pl.when(pl.program_id(0) == 0)
    def _():
        carry_ref[...] = jnp.zeros_like(carry_ref)

    z = f_ref[...] + b_ref[...]
    logf = jnp.minimum(z, 0.0) - jnp.log1p(jnp.exp(-jnp.abs(z)))
    row = lax.broadcasted_iota(I32, (bs, bs), 0)
    col = lax.broadcasted_iota(I32, (bs, bs), 1)
    tri = jnp.where(col <= row, 1.0, 0.0).astype(BF16)
    hi, mid, lo = _split3(logf)
    cs = (jnp.dot(tri, hi, preferred_element_type=F32) + jnp.dot(tri, mid, preferred_element_type=F32)
          + jnp.dot(tri, lo, preferred_element_type=F32)) + carry_ref[...]
    carry_ref[...] = cs[bs - 1:bs, :]
    c2 = cs * LOG2E
    c_ref[...] = c2
    hi_ref[...], mid_ref[...], lo_ref[...] = _split3(c2)


def _forget_cumsum(f_logit, b_forget_pad, bs=256):
    s, w = f_logit.shape
    bs = min(bs, s)
    blk = lambda: pl.BlockSpec((bs, w), lambda i: (i, 0))
    return pl.pallas_call(
        functools.partial(_forget_cumsum_kernel, bs=bs),
        grid=(s // bs,),
        in_specs=[blk(), pl.BlockSpec((1, w), lambda i: (0, 0))],
        out_specs=[blk(), blk(), blk(), blk()],
        out_shape=[jax.ShapeDtypeStruct((s, w), F32)] + [jax.ShapeDtypeStruct((s, w), BF16)] * 3,
        scratch_shapes=[pltpu.VMEM((1, w), F32)],
        compiler_params=_params("arbitrary"),
        name="forget_cumsum",
    )(f_logit, b_forget_pad)


def _fox_kernel(jstart_ref, qt_ref, qb_ref, k_ref, kb_ref, vt_ref, o_ref, m_ref, l_ref, acc_ref, *, nb, blk):
    h, i = pl.program_id(0), pl.program_id(1)
    qt = jnp.concatenate([qt_ref[...], qb_ref[...]], axis=0)

    def rows(j):
        return pl.ds(pl.multiple_of(j * blk, blk), blk)

    def scores(j):
        kj = jnp.concatenate([k_ref[rows(j), :], kb_ref[rows(j), :]], axis=1)
        return jnp.dot(kj, qt, preferred_element_type=F32)

    st = scores(i)
    kpos = lax.broadcasted_iota(I32, st.shape, 0)
    qpos = lax.broadcasted_iota(I32, st.shape, 1)
    st = jnp.where(kpos <= qpos, st, -jnp.inf)
    m0 = jnp.max(st, axis=0, keepdims=True)
    p = jnp.exp2(st - m0)
    m_ref[...] = m0
    l_ref[...] = jnp.sum(p, axis=0, keepdims=True)
    acc_ref[...] = jnp.dot(vt_ref[:, rows(i)], p.astype(BF16), preferred_element_type=F32)

    def update(state, j, st):
        m_old, l_old, acc_old = state
        m_new = jnp.maximum(m_old, jnp.max(st, axis=0, keepdims=True))
        p = jnp.exp2(st - m_new)
        corr = jnp.exp2(m_old - m_new)
        return (m_new, l_old * corr + jnp.sum(p, axis=0, keepdims=True),
                acc_old * corr + jnp.dot(vt_ref[:, rows(j)], p.astype(BF16), preferred_element_type=F32))

    def visit(*blocks):
        sts = [scores(j) for j in blocks]
        state = (m_ref[...], l_ref[...], acc_ref[...])
        for j, st in zip(blocks, sts):
            state = update(state, j, st)
        m_ref[...], l_ref[...], acc_ref[...] = state

    def quad(t, carry):
        j = i - 1 - 4 * t
        visit(j, j - 1, j - 2, j - 3)
        return carry

    n_prev = i - jstart_ref[h * nb + i]
    lax.fori_loop(0, n_prev // 4, quad, 0)
    j_rest = i - 1 - (n_prev // 4) * 4

    @pl.when(n_prev % 4 >= 2)
    def _():
        visit(j_rest, j_rest - 1)

    @pl.when(n_prev % 2 == 1)
    def _():
        visit(i - n_prev)
    o_ref[...] = (acc_ref[...] * (1.0 / l_ref[...])).T.astype(o_ref.dtype)


def _fox_attention(qt, k, vt, c2, c_pieces, logit_bound, n_heads, dh, blk=512):
    s = k.shape[0]
    blk = min(blk, s)
    nb = s // blk
    pieces = jnp.stack([jnp.transpose(p[:, :n_heads]) for p in c_pieces], axis=1)
    ones = jnp.ones_like(pieces)
    qb = jnp.pad(jnp.concatenate([pieces, ones], axis=1), ((0, 0), (0, dh - 6), (0, 0)))
    kb = jnp.pad(jnp.transpose(jnp.concatenate([ones, -pieces], axis=1), (0, 2, 1)),
                 ((0, 0), (0, 0), (0, dh - 6)))

    ct = jnp.transpose(c2[:, :n_heads]).reshape(n_heads, nb, blk)
    max_bias = ct[:, :, None, 0] - ct[:, None, :, blk - 1]
    ii = jnp.arange(nb)
    needed = jnp.logical_and(max_bias >= -(2.0 * logit_bound + EXP2_UNDERFLOW), ii[None, None, :] <= ii[None, :, None])
    jstart = (ii[None, :] + 1 - jnp.sum(needed, axis=-1)).astype(I32).reshape(n_heads * nb)

    return pl.pallas_call(
        functools.partial(_fox_kernel, nb=nb, blk=blk),
        grid_spec=pltpu.PrefetchScalarGridSpec(
            num_scalar_prefetch=1,
            grid=(n_heads, nb),
            in_specs=[
                pl.BlockSpec((dh, blk), lambda h, i, js: (h, i)),
                pl.BlockSpec((None, dh, blk), lambda h, i, js: (h, 0, i)),
                pl.BlockSpec((s, dh), lambda h, i, js: (0, h)),
                pl.BlockSpec((None, s, dh), lambda h, i, js: (h, 0, 0)),
                pl.BlockSpec((dh, s), lambda h, i, js: (h, 0)),
            ],
            out_specs=pl.BlockSpec((blk, dh), lambda h, i, js: (i, h)),
            scratch_shapes=[pltpu.VMEM((1, blk), F32), pltpu.VMEM((1, blk), F32), pltpu.VMEM((dh, blk), F32)],
        ),
        out_shape=jax.ShapeDtypeStruct((s, n_heads * dh), BF16),
        compiler_params=_params("parallel", "arbitrary"),
        name="fox_attention",
    )(jstart, qt, qb, k, kb, vt)


def _s5_params_kernel(lre_ref, lim_ref, ls_ref, bre_ref, bim_ref, ctre_ref, ctim_ref, cre_ref, cim_ref,
                      zre_ref, zim_ref, wre_ref, wim_ref, m_ref, alre_ref, alim_ref, *, chunk, n_ch):
    lre, lim = lre_ref[...], lim_ref[...]
    step = jnp.exp(ls_ref[...])
    ar, ai = lre * step, lim * step
    lc = chunk * n_ch
    tau = (lax.broadcasted_iota(I32, (1, 1, lc), 2) // n_ch).astype(F32)

    def cexp(t):
        mag = jnp.exp(ar * t)
        return mag * jnp.cos(ai * t), mag * jnp.sin(ai * t)

    e1r, e1i = cexp(1.0)
    nr, ni = e1r - 1.0, e1i
    den = lre * lre + lim * lim
    cr, ci = (nr * lre + ni * lim) / den, (ni * lre - nr * lim) / den
    bre, bim = bre_ref[...], bim_ref[...]
    bbr, bbi = cr * bre - ci * bim, cr * bim + ci * bre
    er, ei = cexp(tau)
    zr, zi = er * bbr - ei * bbi, er * bbi + ei * bbr
    zre_ref[...] = zr
    zim_ref[...] = zi
    e1r_t, e1i_t = cexp(tau + 1.0)
    ctre, ctim = ctre_ref[...], ctim_ref[...]
    wre_ref[...] = ctre * e1r_t - ctim * e1i_t
    wim_ref[...] = ctre * e1i_t + ctim * e1r_t
    dn = (((2,), (1,)), ((0,), (0,)))
    m_ref[...] = (lax.dot_general(cre_ref[...], zr, dn, precision=HIGHEST, preferred_element_type=F32)
                  - lax.dot_general(cim_ref[...], zi, dn, precision=HIGHEST, preferred_element_type=F32))
    alr, ali = cexp(float(chunk))
    alre_ref[...] = alr
    alim_ref[...] = ali


def _s5_operators(lam_re, lam_im, log_step, b_re, b_im, c_re, c_im, chunk):
    g, n = lam_re.shape
    n_ch = b_re.shape[-1]
    gs = LANES // n_ch
    assert g % gs == 0
    nsl, lc = g // gs, chunk * n_ch
    tile = lambda a: jnp.tile(a, (1, 1, chunk))
    ct_re, ct_im = jnp.transpose(c_re, (0, 2, 1)), jnp.transpose(c_im, (0, 2, 1))
    spec3 = lambda a, b: pl.BlockSpec((gs, a, b), lambda i: (i, 0, 0))
    zre, zim, wre, wim, mk, alre, alim = pl.pallas_call(
        functools.partial(_s5_params_kernel, chunk=chunk, n_ch=n_ch),
        grid=(nsl,),
        in_specs=[spec3(n, 1), spec3(n, 1), spec3(1, 1), spec3(n, lc), spec3(n, lc), spec3(n, lc), spec3(n, lc),
                  spec3(n_ch, n), spec3(n_ch, n)],
        out_specs=[spec3(n, lc), spec3(n, lc), spec3(n, lc), spec3(n, lc), spec3(n_ch, lc), spec3(n, 1), spec3(n, 1)],
        out_shape=[jax.ShapeDtypeStruct((g, n, lc), F32)] * 4 + [jax.ShapeDtypeStruct((g, n_ch, lc), F32)]
        + [jax.ShapeDtypeStruct((g, n, 1), F32)] * 2,
        compiler_params=_params("parallel"),
        name="s5_params",
    )(lam_re.reshape(g, n, 1), lam_im.reshape(g, n, 1), log_step.reshape(g, 1, 1), tile(b_re), tile(b_im),
      tile(ct_re), tile(ct_im), c_re, c_im)

    eye = jnp.eye(gs, dtype=F32)
    m5 = mk.reshape(nsl, gs, n_ch, chunk, n_ch)
    kk = jnp.transpose(m5, (0, 3, 1, 4, 2))
    kk = (kk[:, :, :, :, None, :] * eye[None, None, :, None, :, None]).reshape(nsl, chunk, LANES, LANES)
    d = jnp.arange(chunk)[None, :] - jnp.arange(chunk)[:, None]
    tm = jnp.where((d >= 0)[None, :, :, None, None], kk[:, jnp.clip(d, 0, chunk - 1)], 0.0)
    t_op = jnp.transpose(tm, (0, 1, 3, 2, 4)).reshape(nsl, chunk * LANES, chunk * LANES).astype(BF16)

    def inject(z):
        z5 = z.reshape(nsl, gs, n, chunk, n_ch)[:, :, :, ::-1, :]
        pd = jnp.transpose(z5, (0, 3, 1, 4, 2))
        return (pd[:, :, :, :, None, :] * eye[None, None, :, None, :, None]).reshape(nsl, chunk * LANES, gs * n)

    def readout(w):
        w5 = w.reshape(nsl, gs, n, chunk, n_ch)
        return (w5[:, :, :, :, None, :] * eye[None, :, None, None, :, None]).reshape(nsl, gs * n, chunk * LANES)

    p_op = jnp.concatenate([inject(zre), inject(zim)], axis=-1).astype(BF16)
    q_op = jnp.concatenate([readout(wre), -readout(wim)], axis=1).astype(BF16)
    a_l = jnp.concatenate([alre.reshape(nsl, 1, gs * n), alim.reshape(nsl, 1, gs * n)], axis=-1)
    return t_op, p_op, q_op, a_l


def _gelu_tanh(x):
    return 0.5 * x * (1.0 + jnp.tanh(math.sqrt(2.0 / math.pi) * (x + 0.044715 * (x * x * x))))


def _s5_conv_kernel(u_ref, t_ref, p_ref, q_ref, a_ref, d_ref, o_ref, hst_ref, s_ref, hp_ref, *, chunk, bk):
    hn = a_ref.shape[1] // 2

    @pl.when(pl.program_id(1) == 0)
    def _():
        hst_ref[...] = jnp.zeros_like(hst_ref)

    x = jnp.concatenate([u_ref[pl.ds(s, bk, stride=chunk), :] for s in range(chunk)], axis=1).astype(BF16)
    s_ref[...] = jnp.dot(x, p_ref[...], preferred_element_type=F32)
    ar, ai = a_ref[:, :hn], a_ref[:, hn:]

    def body(k, h):
        hr, hi = h
        hp_ref[pl.ds(k, 1), :hn] = hr
        hp_ref[pl.ds(k, 1), hn:] = hi
        row = s_ref[pl.ds(k, 1), :]
        return ar * hr - ai * hi + row[:, :hn], ar * hi + ai * hr + row[:, hn:]

    hr, hi = lax.fori_loop(0, bk, body, (hst_ref[:, :hn], hst_ref[:, hn:]))
    hst_ref[:, :hn] = hr
    hst_ref[:, hn:] = hi

    y = (jnp.dot(x, t_ref[...], preferred_element_type=F32)
         + jnp.dot(hp_ref[...].astype(BF16), q_ref[...], preferred_element_type=F32))
    d = d_ref[...]
    for t in range(chunk):
        yt = y[:, t * LANES:(t + 1) * LANES] + d * u_ref[pl.ds(t, bk, stride=chunk), :]
        o_ref[pl.ds(t, bk, stride=chunk), :] = _gelu_tanh(yt)


def _s5_conv(u, t_op, p_op, q_op, a_l, d_skip, chunk, bk=512):
    s, p = u.shape
    nsl = p // LANES
    bk = min(bk, s // chunk)
    bm = bk * chunk
    hn2 = a_l.shape[-1]
    return pl.pallas_call(
        functools.partial(_s5_conv_kernel, chunk=chunk, bk=bk),
        grid=(nsl, s // bm),
        in_specs=[
            pl.BlockSpec((bm, LANES), lambda sl, i: (i, sl)),
            pl.BlockSpec((None, chunk * LANES, chunk * LANES), lambda sl, i: (sl, 0, 0)),
            pl.BlockSpec((None, chunk * LANES, hn2), lambda sl, i: (sl, 0, 0)),
            pl.BlockSpec((None, hn2, chunk * LANES), lambda sl, i: (sl, 0, 0)),
            pl.BlockSpec((None, 1, hn2), lambda sl, i: (sl, 0, 0)),
            pl.BlockSpec((1, LANES), lambda sl, i: (0, sl)),
        ],
        out_specs=pl.BlockSpec((bm, LANES), lambda sl, i: (i, sl)),
        out_shape=jax.ShapeDtypeStruct((s, p), F32),
        scratch_shapes=[pltpu.VMEM((1, hn2), F32), pltpu.VMEM((bk, hn2), F32), pltpu.VMEM((bk, hn2), F32)],
        compiler_params=_params("parallel", "arbitrary"),
        name="s5_conv",
    )(u, t_op, p_op, q_op, a_l, d_skip.reshape(1, p))


def _mem_attn_kernel(q_ref, k_ref, v_ref, o_ref, *, n_heads, dm):
    for h in range(n_heads):
        sl = slice(h * dm, (h + 1) * dm)
        s = lax.dot_general(q_ref[:, sl], k_ref[:, sl], (((1,), (1,)), ((), ())), preferred_element_type=F32)
        p = jnp.exp(s - jnp.max(s, axis=-1, keepdims=True))
        o = jnp.dot(p.astype(BF16), v_ref[:, sl], preferred_element_type=F32) / jnp.sum(p, axis=-1, keepdims=True)
        o_ref[:, sl] = o.astype(o_ref.dtype)


def _mem_attention(q, k, v, n_heads, dm, bm=512):
    s, w = q.shape
    m = k.shape[0]
    bm = min(bm, s)
    return pl.pallas_call(
        functools.partial(_mem_attn_kernel, n_heads=n_heads, dm=dm),
        grid=(s // bm,),
        in_specs=[pl.BlockSpec((bm, w), lambda i: (i, 0)), pl.BlockSpec((m, w), lambda i: (0, 0)),
                  pl.BlockSpec((m, w), lambda i: (0, 0))],
        out_specs=pl.BlockSpec((bm, w), lambda i: (i, 0)),
        out_shape=jax.ShapeDtypeStruct((s, w), BF16),
        compiler_params=_params("parallel"),
        name="mem_attention",
    )(q, k, v)


def _merge_kernel(ya_ref, ys_ref, ym_ref, wa_ref, ws_ref, wm_ref, ga_ref, gs_ref, gm_ref, o_ref):
    a = jnp.dot(ya_ref[...], wa_ref[...], preferred_element_type=F32)
    s = jnp.dot(ys_ref[...], ws_ref[...], preferred_element_type=F32)
    m = jnp.dot(ym_ref[...], wm_ref[...], preferred_element_type=F32)
    o_ref[...] = (ga_ref[...].astype(F32) * a + gs_ref[...].astype(F32) * s
                  + gm_ref[...].astype(F32) * m).astype(o_ref.dtype)


def _merge(ya, ys, ym, wa, ws, wm, gates, bm=1024, bn=512):
    s, d = ya.shape[0], wa.shape[1]
    bm, bn = min(bm, s), min(bn, d)
    nb = d // bn
    row = lambda a: pl.BlockSpec((bm, a.shape[1]), lambda i, j: (i, 0))
    col = lambda a: pl.BlockSpec((a.shape[0], bn), lambda i, j: (0, j))
    gate = lambda b: pl.BlockSpec((bm, bn), lambda i, j, b=b: (i, j + b * nb))
    return pl.pallas_call(
        _merge_kernel,
        grid=(s // bm, nb),
        in_specs=[row(ya), row(ys), row(ym), col(wa), col(ws), col(wm), gate(0), gate(1), gate(2)],
        out_specs=pl.BlockSpec((bm, bn), lambda i, j: (i, j)),
        out_shape=jax.ShapeDtypeStruct((s, d), BF16),
        compiler_params=_params("parallel", "parallel"),
        name="merge",
    )(ya, ys, ym, wa, ws, wm, gates, gates, gates)


def _router_kernel(x_ref, g_ref, wr_ref, br_ref, xp_ref, idx_ref, gate_ref, rank_ref, cnt_ref, carry_ref, *, bm):
    @pl.when(pl.program_id(0) == 0)
    def _():
        carry_ref[...] = jnp.zeros_like(carry_ref)

    x = x_ref[...]
    ms = jnp.mean(x * x, axis=-1, keepdims=True)
    xn = x * lax.rsqrt(ms + RMS_EPS) * g_ref[...]
    d2 = xn.shape[1] // 2
    xp_ref[...] = _pack_bf16_pair(xn[:, :d2], xn[:, d2:])

    n_exp = wr_ref.shape[1]
    x_hi, w_hi = xn.astype(BF16), wr_ref[...].astype(BF16)
    x_lo = (xn - x_hi.astype(F32)).astype(BF16)
    w_lo = (wr_ref[...] - w_hi.astype(F32)).astype(BF16)
    logits = (jnp.dot(x_hi, w_hi, preferred_element_type=F32) + jnp.dot(x_lo, w_hi, preferred_element_type=F32)
              + jnp.dot(x_hi, w_lo, preferred_element_type=F32)) + br_ref[...]
    lane = lax.broadcasted_iota(I32, logits.shape, 1)
    kcol = lax.broadcasted_iota(I32, (bm, TOP_K), 1)
    work = logits
    member = jnp.zeros_like(logits)
    idxs, vals = [], []
    for _ in range(TOP_K):
        mx = jnp.max(work, axis=-1, keepdims=True)
        ik = jnp.min(jnp.where(work == mx, lane, n_exp), axis=-1, keepdims=True)
        sel = lane == ik
        member = jnp.where(sel, 1.0, member)
        work = jnp.where(sel, -jnp.inf, work)
        idxs.append(ik)
        vals.append(mx)
    es = [jnp.exp(v - vals[0]) for v in vals]
    den = es[0] + es[1] + es[2] + es[3]

    r = lax.broadcasted_iota(I32, (bm, bm), 0)
    c = lax.broadcasted_iota(I32, (bm, bm), 1)
    tri = jnp.where(c < r, 1.0, 0.0).astype(BF16)
    prefix = jnp.dot(tri, member.astype(BF16), preferred_element_type=F32) + carry_ref[...]
    total = carry_ref[...] + jnp.sum(member, axis=0, keepdims=True)
    carry_ref[...] = total
    cnt_ref[...] = total

    idx_o = jnp.zeros((bm, TOP_K), I32)
    gate_o = jnp.zeros((bm, TOP_K), F32)
    rank_o = jnp.zeros((bm, TOP_K), F32)
    for k in range(TOP_K):
        rk = jnp.sum(jnp.where(lane == idxs[k], prefix, 0.0), axis=-1, keepdims=True)
        idx_o = jnp.where(kcol == k, idxs[k], idx_o)
        gate_o = jnp.where(kcol == k, es[k] / den, gate_o)
        rank_o = jnp.where(kcol == k, rk, rank_o)
    idx_ref[...] = idx_o
    gate_ref[...] = gate_o
    rank_ref[...] = rank_o.astype(I32)


def _router(x1, g, w_router, b_router, bm=256):
    t, d = x1.shape
    n_exp = w_router.shape[1]
    bm = min(bm, t)
    small = lambda: pl.BlockSpec((bm, TOP_K), lambda i: (i, 0))
    return pl.pallas_call(
        functools.partial(_router_kernel, bm=bm),
        grid=(t // bm,),
        in_specs=[pl.BlockSpec((bm, d), lambda i: (i, 0)), pl.BlockSpec((1, d), lambda i: (0, 0)),
                  pl.BlockSpec((d, n_exp), lambda i: (0, 0)), pl.BlockSpec((1, n_exp), lambda i: (0, 0))],
        out_specs=[pl.BlockSpec((bm, d // 2), lambda i: (i, 0)), small(), small(), small(),
                   pl.BlockSpec((1, n_exp), lambda i: (0, 0))],
        out_shape=[jax.ShapeDtypeStruct((t, d // 2), U32), jax.ShapeDtypeStruct((t, TOP_K), I32),
                   jax.ShapeDtypeStruct((t, TOP_K), F32), jax.ShapeDtypeStruct((t, TOP_K), I32),
                   jax.ShapeDtypeStruct((1, n_exp), F32)],
        scratch_shapes=[pltpu.VMEM((1, n_exp), F32)],
        compiler_params=_params("arbitrary"),
        name="router",
    )(x1, g.reshape(1, d), w_router, b_router.reshape(1, n_exp))


def _dispatch_kernel(fill_ref, dest_ref, xp_ref, xs_ref, zero_ref, sem_z, sem_r, *, bm, bm_e, n_tiles):
    @pl.when(pl.program_id(0) == 0)
    def _():
        zero_ref[...] = jnp.zeros_like(zero_ref)

        def fill(t, wait):
            @pl.when(fill_ref[t] > 0)
            def _():
                cp = pltpu.make_async_copy(zero_ref, xs_ref.at[pl.ds(t * bm_e, bm_e), :], sem_z)
                cp.wait() if wait else cp.start()

        lax.fori_loop(0, n_tiles, lambda t, c: (fill(t, False), c)[1], 0)
        lax.fori_loop(0, n_tiles, lambda t, c: (fill(t, True), c)[1], 0)

    def row_copy(r, k):
        return pltpu.make_async_copy(xp_ref.at[pl.ds(r, 1), :], xs_ref.at[pl.ds(dest_ref[r * TOP_K + k], 1), :], sem_r)

    def issue(r, c):
        for k in range(TOP_K):
            row_copy(r, k).start()
        return c

    def drain(r, c):
        for k in range(TOP_K):
            row_copy(r, k).wait()
        return c

    lax.fori_loop(0, bm, issue, 0)
    lax.fori_loop(0, bm, drain, 0)


def _dispatch(xp, dest_flat, tile_fill, bm_e, bm=256):
    t, d2 = xp.shape
    bm = min(bm, t)
    n_tiles = tile_fill.shape[0]
    n_rows = n_tiles * bm_e
    return pl.pallas_call(
        functools.partial(_dispatch_kernel, bm=bm, bm_e=bm_e, n_tiles=n_tiles),
        grid_spec=pltpu.PrefetchScalarGridSpec(
            num_scalar_prefetch=1,
            grid=(t // bm,),
            in_specs=[pl.BlockSpec((bm * TOP_K,), lambda i, lt: (i,), memory_space=pltpu.SMEM),
                      pl.BlockSpec((bm, d2), lambda i, lt: (i, 0))],
            out_specs=pl.BlockSpec(memory_space=pl.ANY),
            scratch_shapes=[pltpu.VMEM((bm_e, d2), U32), pltpu.SemaphoreType.DMA(()), pltpu.SemaphoreType.DMA(())],
        ),
        out_shape=jax.ShapeDtypeStruct((n_rows, d2), U32),
        compiler_params=_params("arbitrary"),
        name="moe_dispatch",
    )(tile_fill, dest_flat, xp)


FIRST_VISIT, TILE_USED = 1, 2


def _work_list(tile_start, tile_count, n_used, n_tiles, n_chunks):
    n_exp = tile_start.shape[0]
    w = jnp.arange(n_tiles * n_chunks, dtype=I32)
    e = jnp.minimum(jnp.sum((n_chunks * (tile_start + tile_count))[None, :] <= w[:, None], axis=1), n_exp - 1).astype(I32)
    cnt = jnp.maximum(tile_count[e], 1)
    r = w - n_chunks * tile_start[e]
    used = w < n_chunks * n_used
    n_unused = jnp.maximum(n_tiles - n_used, 1)
    r2 = w - n_chunks * n_used
    tile = jnp.where(used, tile_start[e] + r % cnt, n_used + r2 % n_unused)
    chunk = jnp.where(used, r // cnt, r2 // n_unused)
    last = n_chunks * n_used - 1
    in_tile = jnp.where(used, tile, tile[last])
    w_exp = jnp.where(used, e, e[last])
    w_chunk = jnp.where(used, chunk, chunk[last])
    flags = jnp.where(used, TILE_USED + jnp.where(r % cnt == 0, FIRST_VISIT, 0), 0)
    return [a.astype(I32) for a in (tile, chunk, in_tile, w_exp, w_chunk, flags)]


def _moe_up_kernel(tile_ref, chunk_ref, xin_ref, we_ref, wc_ref, flag_ref, xs_ref, wg_ref, wl_ref, bg_ref, bl_ref,
                   o_ref, wb_ref):
    flag = flag_ref[pl.program_id(0)]
    tf = wg_ref.shape[1]

    @pl.when((flag & FIRST_VISIT) != 0)
    def _():
        wb_ref[:, :tf] = wg_ref[...].astype(BF16)
        wb_ref[:, tf:] = wl_ref[...].astype(BF16)

    @pl.when((flag & TILE_USED) != 0)
    def _():
        lo, hi = _unpack_bf16_pair(xs_ref[...])
        xlo, xhi = lo.astype(BF16), hi.astype(BF16)
        d2 = xlo.shape[1]
        hcat = (jnp.dot(xlo, wb_ref[:d2, :], preferred_element_type=F32)
                + jnp.dot(xhi, wb_ref[d2:, :], preferred_element_type=F32))
        glu = jnp.minimum(hcat[:, :tf] + bg_ref[...], SWIGLU_LIMIT)
        lin = jnp.clip(hcat[:, tf:] + bl_ref[...], -SWIGLU_LIMIT, SWIGLU_LIMIT)
        o_ref[...] = (glu * jax.nn.sigmoid(SWIGLU_ALPHA * glu) * (lin + 1.0)).astype(o_ref.dtype)

    @pl.when((flag & TILE_USED) == 0)
    def _():
        o_ref[...] = jnp.zeros_like(o_ref)


def _moe_up(xs, work, w1, b1, bm_e, tf):
    n_rows, d2 = xs.shape
    n_exp, d, f2 = w1.shape
    ff = f2 // 2
    nf = ff // tf
    wspec = lambda off: pl.BlockSpec((None, d, tf), lambda w, t, c, xi, we, wc, fl: (we[w], 0, wc[w] + off))
    bspec = lambda off: pl.BlockSpec((None, 1, tf), lambda w, t, c, xi, we, wc, fl: (we[w], 0, wc[w] + off))
    return pl.pallas_call(
        _moe_up_kernel,
        grid_spec=pltpu.PrefetchScalarGridSpec(
            num_scalar_prefetch=6,
            grid=(work[0].shape[0],),
            in_specs=[pl.BlockSpec((bm_e, d2), lambda w, t, c, xi, we, wc, fl: (xi[w], 0)),
                      wspec(0), wspec(nf), bspec(0), bspec(nf)],
            out_specs=pl.BlockSpec((bm_e, tf), lambda w, t, c, xi, we, wc, fl: (t[w], c[w])),
            scratch_shapes=[pltpu.VMEM((d, 2 * tf), BF16)],
        ),
        out_shape=jax.ShapeDtypeStruct((n_rows, ff), BF16),
        compiler_params=_params("arbitrary"),
        name="moe_up",
    )(*work, xs, w1, w1, b1.reshape(n_exp, 1, f2), b1.reshape(n_exp, 1, f2))


def _moe_down_kernel(tile_ref, chunk_ref, hin_ref, we_ref, wc_ref, flag_ref, h_ref, w2_ref, b2_ref, o_ref, w2b_ref):
    flag = flag_ref[pl.program_id(0)]

    @pl.when((flag & FIRST_VISIT) != 0)
    def _():
        w2b_ref[...] = w2_ref[...].astype(BF16)

    @pl.when((flag & TILE_USED) != 0)
    def _():
        y = jnp.dot(h_ref[...], w2b_ref[...], preferred_element_type=F32) + b2_ref[...]
        half = y.shape[1] // 2
        o_ref[...] = _pack_bf16_pair(y[:, :half], y[:, half:])

    @pl.when((flag & TILE_USED) == 0)
    def _():
        o_ref[...] = jnp.zeros_like(o_ref)


def _moe_down(hid, work, w2, b2, bm_e, tn):
    n_rows, ff = hid.shape
    n_exp, _, d = w2.shape
    return pl.pallas_call(
        _moe_down_kernel,
        grid_spec=pltpu.PrefetchScalarGridSpec(
            num_scalar_prefetch=6,
            grid=(work[0].shape[0],),
            in_specs=[pl.BlockSpec((bm_e, ff), lambda w, t, c, hi, we, wc, fl: (hi[w], 0)),
                      pl.BlockSpec((None, ff, tn), lambda w, t, c, hi, we, wc, fl: (we[w], 0, wc[w])),
                      pl.BlockSpec((None, 1, tn), lambda w, t, c, hi, we, wc, fl: (we[w], 0, wc[w]))],
            out_specs=pl.BlockSpec((bm_e, tn // 2), lambda w, t, c, hi, we, wc, fl: (t[w], c[w])),
            scratch_shapes=[pltpu.VMEM((ff, tn), BF16)],
        ),
        out_shape=jax.ShapeDtypeStruct((n_rows, d // 2), U32),
        compiler_params=_params("arbitrary"),
        name="moe_down",
    )(*work, hid, w2, b2.reshape(n_exp, 1, d))


def _combine_kernel(dest_ref, x_ref, gate_ref, ys_ref, o_ref, buf_ref, sem, *, bm, tn):
    def row_copy(r, k):
        return pltpu.make_async_copy(ys_ref.at[pl.ds(dest_ref[r * TOP_K + k], 1), :],
                                     buf_ref.at[k, pl.ds(r, 1), :], sem)

    def issue(r, c):
        for k in range(TOP_K):
            row_copy(r, k).start()
        return c

    def drain(r, c):
        for k in range(TOP_K):
            row_copy(r, k).wait()
        return c

    lax.fori_loop(0, bm, issue, 0)
    lax.fori_loop(0, bm, drain, 0)
    g = gate_ref[...]
    half = tn // 2
    for n in range(x_ref.shape[1] // tn):
        acc_lo = x_ref[:, n * tn:n * tn + half]
        acc_hi = x_ref[:, n * tn + half:(n + 1) * tn]
        for k in range(TOP_K):
            lo, hi = _unpack_bf16_pair(buf_ref[k, :, n * half:(n + 1) * half])
            acc_lo = acc_lo + g[:, k:k + 1] * lo
            acc_hi = acc_hi + g[:, k:k + 1] * hi
        o_ref[:, n * tn:n * tn + half] = acc_lo
        o_ref[:, n * tn + half:(n + 1) * tn] = acc_hi


def _combine(x1, gate, dest_flat, ys, tn, bm=128):
    t, d = x1.shape
    bm = min(bm, t)
    return pl.pallas_call(
        functools.partial(_combine_kernel, bm=bm, tn=tn),
        grid=(t // bm,),
        in_specs=[pl.BlockSpec((bm * TOP_K,), lambda i: (i,), memory_space=pltpu.SMEM),
                  pl.BlockSpec((bm, d), lambda i: (i, 0)),
                  pl.BlockSpec((bm, TOP_K), lambda i: (i, 0)),
                  pl.BlockSpec(memory_space=pl.ANY)],
        out_specs=pl.BlockSpec((bm, d), lambda i: (i, 0)),
        out_shape=jax.ShapeDtypeStruct((t, d), F32),
        scratch_shapes=[pltpu.VMEM((TOP_K, bm, d // 2), U32), pltpu.SemaphoreType.DMA(())],
        compiler_params=_params("arbitrary"),
        name="moe_combine",
    )(dest_flat, x1, gate, ys)


def _mixer(x2, mem2, layer, norm_mix, norm_mem, w_in, b_gate, b_forget, q_norm_attn, k_norm_attn, q_norm_mem, k_norm_mem,
           w_mem_kv, lam_re, lam_im, log_step, ssm_b_re, ssm_b_im, ssm_c_re, ssm_c_im, ssm_d, w_glu, b_glu,
           w_up_attn, w_up_ssm, w_up_mem, w_out):
    s, d = x2.shape
    n_heads, dh = b_forget.shape[0], q_norm_attn.shape[0]
    aw = n_heads * dh
    pw = ssm_d.shape[0]
    dm = q_norm_mem.shape[0]
    mw = w_up_mem.shape[0]
    mem_heads = mw // dm
    o_k, o_v, o_f, o_u, o_qm, o_g = aw, 2 * aw, 3 * aw, 3 * aw + n_heads, 3 * aw + n_heads + pw, 3 * aw + n_heads + pw + mw
    bf = lambda a: a.astype(BF16)

    h = _rmsnorm(x2, norm_mix, BF16)
    w_t = jnp.transpose(w_in, (2, 0, 1))
    w_qkv = _w_in_cols(w_t, layer, 0, o_f, "w_in_qkv")
    w_f = _w_in_cols(w_t, layer, o_f, LANES, "w_in_f", n_valid=n_heads)
    w_gates = _w_in_cols(w_t, layer, o_g, 3 * d, "w_in_gates")
    w_uqm = _w_in_cols(w_t, layer, o_u, pw + mw, "w_in_uqm")
    qt = _matmul(h, w_qkv, [(jnp.tile(q_norm_attn, n_heads).reshape(1, aw), "row", 0)],
                 functools.partial(_ep_headnorm, dh=dh, scale=dh ** -0.5 * LOG2E), BF16, 1024, 512, "in_q",
                 w_cols=(0, aw), transpose_out=True)
    k = _matmul(h, w_qkv, [(jnp.tile(k_norm_attn, n_heads).reshape(1, aw), "row", 0)],
                functools.partial(_ep_headnorm, dh=dh, scale=1.0), BF16, 1024, 512, "in_k", w_cols=(o_k, aw))
    vt = _matmul(h, w_qkv, [], _ep_plain, BF16, 1024, 512, "in_v", w_cols=(o_v, aw), transpose_out=True)
    f_logit = _matmul(h, w_f, [], _ep_plain, F32, 1024, LANES, "in_f")
    gates = _matmul(h, w_gates, [(b_gate.reshape(1, 3 * d), "row", 0)], _ep_sigmoid_bias, BF16, 1024, 1024,
                    "in_gates")
    u = _matmul(h, w_uqm, [], _ep_plain, F32, 1024, 512, "in_u", w_cols=(0, pw))
    qm = _matmul(h, w_uqm, [(jnp.tile(q_norm_mem, mem_heads).reshape(1, mw), "row", 0)],
                 functools.partial(_ep_headnorm, dh=dm, scale=dm ** -0.5), BF16, 1024, 512, "in_qm",
                 w_cols=(pw, mw))

    c2, *c_pieces = _forget_cumsum(f_logit, jnp.pad(b_forget, (0, LANES - n_heads)).reshape(1, LANES))
    logit_bound = 1.02 * LOG2E * dh ** 0.5 * jnp.max(jnp.abs(q_norm_attn)) * jnp.max(jnp.abs(k_norm_attn))
    y_attn = _fox_attention(qt, k, vt, c2, c_pieces, logit_bound, n_heads, dh)

    t_op, p_op, q_op, a_l = _s5_operators(lam_re, lam_im, log_step, ssm_b_re, ssm_b_im, ssm_c_re, ssm_c_im, SSM_CHUNK)
    yg = _s5_conv(u, t_op, p_op, q_op, a_l, ssm_d, SSM_CHUNK)
    y_ssm = _matmul(yg, bf(w_glu), [(b_glu.reshape(1, pw), "row", 0), (yg, "tile", 0)], _ep_glu, BF16, 1024, 512, "s5_glu")

    mem_h = _rmsnorm(mem2, norm_mem, BF16)
    k_m = _matmul(mem_h, bf(w_mem_kv[:, :mw]), [(jnp.tile(k_norm_mem, mem_heads).reshape(1, mw), "row", 0)],
                  functools.partial(_ep_headnorm, dh=dm, scale=1.0), BF16, 256, 512, "mem_k")
    v_m = _matmul(mem_h, bf(w_mem_kv[:, mw:]), [], _ep_plain, BF16, 256, 512, "mem_v")
    y_mem = _mem_attention(qm, k_m, v_m, mem_heads, dm)

    merged = _merge(y_attn, y_ssm, y_mem, bf(w_up_attn), bf(w_up_ssm), bf(w_up_mem), gates)
    return _matmul(merged, bf(w_out), [(x2, "tile", 0)], _ep_residual, F32, 1024, 512, "out_proj")


def _moe(x1, norm_ffn, w_router, b_router, exp_w1, exp_b1, exp_w2, exp_b2, bm_e=512):
    t, d = x1.shape
    n_exp = w_router.shape[1]
    xp, idx, gate, rank, counts = _router(x1, norm_ffn, w_router, b_router)

    counts = counts.reshape(n_exp).astype(I32)
    padded = (counts + bm_e - 1) // bm_e * bm_e
    pend = jnp.cumsum(padded)
    pstart = pend - padded
    dest_flat = (pstart[idx] + rank).reshape(t * TOP_K)
    n_tiles = -(-(t * TOP_K) // bm_e) + n_exp
    n_used = (pend[-1] // bm_e).astype(I32)
    tiles = jnp.arange(n_tiles, dtype=I32)
    has_pad = jnp.zeros((n_tiles,), I32).at[pend // bm_e - 1].max((padded > counts).astype(I32))
    tile_fill = jnp.where(tiles < n_used, has_pad, 1)
    tf = _pick_block(exp_w2.shape[1], MOE_UP_CHUNK)
    tn = _pick_block(d, MOE_DOWN_CHUNK)
    work_up = _work_list(pstart // bm_e, padded // bm_e, n_used, n_tiles, exp_w2.shape[1] // tf)
    work_down = _work_list(pstart // bm_e, padded // bm_e, n_used, n_tiles, d // tn)

    xs = _dispatch(xp, dest_flat, tile_fill, bm_e)
    hid = _moe_up(xs, work_up, exp_w1, exp_b1, bm_e, tf)
    ys = _moe_down(hid, work_down, exp_w2, exp_b2, bm_e, tn)
    return _combine(x1, gate, dest_flat, ys, tn)


def kernel(x, mem, norm_mix, norm_ffn, norm_mem, w_in, b_gate, b_forget, q_norm_attn, k_norm_attn, q_norm_mem, k_norm_mem, w_mem_kv, lam_re, lam_im, log_step, ssm_b_re, ssm_b_im, ssm_c_re, ssm_c_im, ssm_d, w_glu, b_glu, w_up_attn, w_up_ssm, w_up_mem, w_out, w_router, b_router, exp_w1, exp_b1, exp_w2, exp_b2):
    b, s, d = x.shape
    assert b == 1, "one sequence per call"
    x2 = x.reshape(s, d)
    for l in range(norm_mix.shape[0]):
        x2 = _mixer(x2, mem.reshape(mem.shape[1], d), l, norm_mix[l], norm_mem[l], w_in, b_gate[l], b_forget[l],
                    q_norm_attn[l], k_norm_attn[l], q_norm_mem[l], k_norm_mem[l], w_mem_kv[l], lam_re[l], lam_im[l],
                    log_step[l], ssm_b_re[l], ssm_b_im[l], ssm_c_re[l], ssm_c_im[l], ssm_d[l], w_glu[l], b_glu[l],
                    w_up_attn[l], w_up_ssm[l], w_up_mem[l], w_out[l])
        x2 = _moe(x2, norm_ffn[l], w_router[l], b_router[l], exp_w1[l], exp_b1[l], exp_w2[l], exp_b2[l])
    return x2.reshape(b, s, d)
```
